```python
import math
import jax
import jax.numpy as jnp
from jax import lax
import numpy as np

D_MODEL = 2048
BATCH = 2
SEQ = 4096
DEPTH = 2
DEC_BATCH = 32
DEC_SEQ = 8
PAST_LEN = 8192
PAGE_SIZE = 128

DIFF_HEADS = 8
DIFF_DK = 64
DIFF_DV = 2 * DIFF_DK
SB_HEADS = 8
SB_DH = 128
MOBA_HEADS = 16
MOBA_DH = 128
MOBA_BLOCK = 256
MOBA_TOPK = 3
MOBA_QB = 32
ATTN_QB = 128
ROPE_THETA = 10000.0
PEER_HEADS = 8
PEER_NKEYS = 128
PEER_N = PEER_NKEYS * PEER_NKEYS
PEER_DQ = 256
PEER_TOPK = 16
PEER_TB = 128
N_EVEN = (DEPTH + 1) // 2
N_ODD = DEPTH // 2
DN_ALPHA = (2 * DEPTH) ** 0.25
DN_BETA = (8 * DEPTH) ** -0.25
LN_EPS = 1e-5
SUBLN_EPS = 1e-5
DIFF_QK_W = DIFF_HEADS * 2 * DIFF_DK
DIFF_V_W = DIFF_HEADS * DIFF_DV
SB_W = SB_HEADS * SB_DH
EVEN_COLS = (DIFF_QK_W, DIFF_QK_W, DIFF_V_W, SB_W, SB_W, SB_W)
EVEN_MIX_W = DIFF_V_W + SB_W
MOBA_W = MOBA_HEADS * MOBA_DH

kernel_name = 'hybrid_diff_stickbreak_moba_peer_step'


def _layernorm(x, g, b):
    xf = x.astype(jnp.float32)
    mu = jnp.mean(xf, -1, keepdims=True)
    var = jnp.mean(jnp.square(xf - mu), -1, keepdims=True)
    return ((xf - mu) * lax.rsqrt(var + LN_EPS)).astype(x.dtype) * g + b


def _rope(x, pos):
    half = x.shape[-1] // 2
    inv = ROPE_THETA ** (-jnp.arange(half, dtype=jnp.float32) / half)
    ang = pos.astype(jnp.float32)[:, None] * inv[None, :]
    bshape = (pos.shape[0],) + (1,) * (x.ndim - 3) + (half,)
    cos = jnp.cos(ang).reshape(bshape)
    sin = jnp.sin(ang).reshape(bshape)
    xf = x.astype(jnp.float32)
    x1, x2 = xf[..., :half], xf[..., half:]
    return jnp.concatenate([x1 * cos - x2 * sin, x2 * cos + x1 * sin], -1).astype(x.dtype)


def _pad_rows(a, n):
    return jnp.pad(a, [(0, n - a.shape[0])] + [(0, 0)] * (a.ndim - 1))


def _map_query_blocks(fn, qblock, qs, qpos):
    tq = qpos.shape[0]
    qb = min(qblock, tq)
    nb = -(-tq // qb)
    tp = nb * qb
    blocks = tuple(_pad_rows(a, tp).reshape((nb, qb) + a.shape[1:]) for a in qs)
    pos = jnp.concatenate([qpos, jnp.full((tp - tq,), qpos[-1], qpos.dtype)]).reshape(nb, qb)
    out = lax.map(lambda a: fn(*a), blocks + (pos,))
    return out.reshape((tp,) + out.shape[2:])[:tq]


def _gather_pages(cache, layer, pages):
    rows = cache[layer, pages]
    return rows.reshape((-1,) + rows.shape[2:])


def _diff_attn_seq(q, k, v, qpos, lam, lam_init, subln_g):
    kpos = jnp.arange(k.shape[0])

    def block(qb, pb):
        s = jnp.einsum('qhcd,khcd->hcqk', qb, k, preferred_element_type=jnp.float32) * (DIFF_DK ** -0.5)
        s = jnp.where((kpos[None, :] <= pb[:, None])[None, None], s, -jnp.inf)
        p = jax.nn.softmax(s, axis=-1)
        w = p[:, 0] - lam * p[:, 1]
        return jnp.einsum('hqk,khd->qhd', w.astype(v.dtype), v)

    o = _map_query_blocks(block, ATTN_QB, (q,), qpos).astype(jnp.float32)
    o = o * lax.rsqrt(jnp.mean(jnp.square(o), -1, keepdims=True) + SUBLN_EPS)
    return (o * subln_g * (1.0 - lam_init)).astype(v.dtype)


def _stick_break_seq(q, k, v, qpos):
    kpos = jnp.arange(k.shape[0])

    def block(qb, pb):
        z = jnp.einsum('qhd,khd->hqk', qb, k, preferred_element_type=jnp.float32) * (SB_DH ** -0.5)
        past = (kpos[None, :] < pb[:, None])[None]
        log_keep = jnp.where(past, jax.nn.log_sigmoid(-z), 0.0)
        between = lax.cumsum(log_keep, axis=2, reverse=True) - log_keep
        a = jnp.where(past, jnp.exp(jax.nn.log_sigmoid(z) + between), 0.0)
        return jnp.einsum('hqk,khd->qhd', a.astype(v.dtype), v)

    return _map_query_blocks(block, ATTN_QB, (q,), qpos)


def _moba_seq(q, k, v, qpos):
    tk = k.shape[0]
    nb = -(-tk // MOBA_BLOCK)
    kb = _pad_rows(k, nb * MOBA_BLOCK).reshape(nb, MOBA_BLOCK, MOBA_HEADS, MOBA_DH)
    vb = _pad_rows(v, nb * MOBA_BLOCK).reshape(nb, MOBA_BLOCK, MOBA_HEADS, MOBA_DH)
    k_mean = jnp.mean(kb.astype(jnp.float32), axis=1)
    kh = jnp.transpose(kb, (2, 0, 1, 3))
    vh = jnp.transpose(vb, (2, 0, 1, 3))
    head_ix = jnp.arange(MOBA_HEADS)[None, :, None]

    def block(qb, pb):
        n = qb.shape[0]
        cur = pb // MOBA_BLOCK
        gate = jnp.einsum('qhd,nhd->qhn', qb.astype(jnp.float32), k_mean)
        fully_past = jnp.arange(nb)[None, None, :] < cur[:, None, None]
        gate = jnp.where(fully_past, gate, -jnp.inf)
        if nb < MOBA_TOPK:
            gate = jnp.pad(gate, ((0, 0), (0, 0), (0, MOBA_TOPK - nb)), constant_values=-jnp.inf)
        g_val, g_idx = lax.top_k(gate, MOBA_TOPK)
        own = jnp.broadcast_to(cur[:, None, None], (n, MOBA_HEADS, 1))
        blk = jnp.concatenate([jnp.minimum(g_idx, nb - 1), own], -1)
        ok = jnp.concatenate([jnp.isfinite(g_val), jnp.ones(own.shape, bool)], -1)
        kg = kh[head_ix, blk]
        vg = vh[head_ix, blk]
        s = jnp.einsum('qhd,qhnbd->qhnb', qb, kg, preferred_element_type=jnp.float32) * (MOBA_DH ** -0.5)
        kpos = blk[..., None] * MOBA_BLOCK + jnp.arange(MOBA_BLOCK)
        mask = ok[..., None] & (kpos <= pb[:, None, None, None])
        s = jnp.where(mask, s, -jnp.inf)
        p = jax.nn.softmax(s.reshape(n, MOBA_HEADS, -1), axis=-1).reshape(s.shape)
        return jnp.einsum('qhnb,qhnbd->qhd', p.astype(v.dtype), vg)

    return _map_query_blocks(block, MOBA_QB, (q,), qpos)


def _even_qkv(x, pos, w_in):
    b, t, _ = x.shape
    cuts = [int(c) for c in np.cumsum(EVEN_COLS)[:-1]]
    qd, kd, vd, qs, ks, vs = jnp.split(x @ w_in, cuts, axis=-1)
    qd = _rope(qd.reshape(b, t, DIFF_HEADS, 2, DIFF_DK), pos)
    kd = _rope(kd.reshape(b, t, DIFF_HEADS, 2, DIFF_DK), pos)
    vd = vd.reshape(b, t, DIFF_HEADS, DIFF_DV)
    sh = (b, t, SB_HEADS, SB_DH)
    return qd, kd, vd, qs.reshape(sh), ks.reshape(sh), vs.reshape(sh)


def _even_attend_seq(qd, kd, vd, qs, ks, vs, qpos, lam, lam_init, subln_g):
    t = qpos.shape[0]
    od = _diff_attn_seq(qd, kd, vd, qpos, lam, lam_init, subln_g)
    osb = _stick_break_seq(qs, ks, vs, qpos)
    return jnp.concatenate([od.reshape(t, DIFF_V_W), osb.reshape(t, SB_W)], -1)


def _even_prompt(x, pos, w_in, w_out, lam, lam_init, subln_g):
    qd, kd, vd, qs, ks, vs = _even_qkv(x, pos, w_in)
    o = lax.map(lambda a: _even_attend_seq(*a, pos, lam, lam_init, subln_g), (qd, kd, vd, qs, ks, vs))
    return o @ w_out, (kd, vd, ks, vs)


def _even_sample(x, pos, page_table, c_dk, c_dv, c_sk, c_sv, li, w_in, w_out, lam, lam_init, subln_g):
    qd, kd, vd, qs, ks, vs = _even_qkv(x, pos, w_in)

    def one(a):
        qd_b, kd_b, vd_b, qs_b, ks_b, vs_b, pages = a
        full = lambda cache, new: jnp.concatenate([_gather_pages(cache, li, pages), new], 0)
        return _even_attend_seq(qd_b, full(c_dk, kd_b), full(c_dv, vd_b), qs_b,
                                full(c_sk, ks_b), full(c_sv, vs_b), pos, lam, lam_init, subln_g)

    o = lax.map(one, (qd, kd, vd, qs, ks, vs, page_table))
    return o @ w_out, (kd, vd, ks, vs)


def _odd_qkv(x, pos, w_in):
    b, t, _ = x.shape
    q, k, v = jnp.split(x @ w_in, 3, axis=-1)
    sh = (b, t, MOBA_HEADS, MOBA_DH)
    return _rope(q.reshape(sh), pos), _rope(k.reshape(sh), pos), v.reshape(sh)


def _odd_prompt(x, pos, w_in, w_out):
    b, t, _ = x.shape
    q, k, v = _odd_qkv(x, pos, w_in)
    o = lax.map(lambda a: _moba_seq(a[0], a[1], a[2], pos), (q, k, v))
    return o.reshape(b, t, MOBA_W) @ w_out, (k, v)


def _odd_sample(x, pos, page_table, c_k, c_v, li, w_in, w_out):
    b, t, _ = x.shape
    q, k, v = _odd_qkv(x, pos, w_in)

    def one(a):
        q_b, k_b, v_b, pages = a
        full = lambda cache, new: jnp.concatenate([_gather_pages(cache, li, pages), new], 0)
        return _moba_seq(q_b, full(c_k, k_b), full(c_v, v_b), pos)

    o = lax.map(one, (q, k, v, page_table))
    return o.reshape(b, t, MOBA_W) @ w_out, (k, v)


def _peer(x, w_q, subkeys, u, v):
    shp = x.shape
    xt = x.reshape(-1, shp[-1])
    n_tok = xt.shape[0]
    tb = min(PEER_TB, n_tok)
    nb = -(-n_tok // tb)
    xb = _pad_rows(xt, nb * tb).reshape(nb, tb, shp[-1])

    def block(xc):
        q = (xc @ w_q).reshape(tb, PEER_HEADS, 2, PEER_DQ // 2)
        s = jnp.einsum('thpd,pnd->thpn', q, subkeys, preferred_element_type=jnp.float32)
        sv, si = lax.top_k(s, PEER_TOPK)
        cand = sv[:, :, 0, :, None] + sv[:, :, 1, None, :]
        cidx = si[:, :, 0, :, None] * PEER_NKEYS + si[:, :, 1, None, :]
        cv, ci = lax.top_k(cand.reshape(tb, PEER_HEADS, -1), PEER_TOPK)
        eidx = jnp.take_along_axis(cidx.reshape(tb, PEER_HEADS, -1), ci, axis=-1)
        g = jax.nn.softmax(cv, axis=-1)
        ug = u[eidx]
        vg = v[eidx]
        h = jnp.einsum('td,thkd->thk', xc, ug, preferred_element_type=jnp.float32)
        a = jax.nn.gelu(h, approximate=False) * g
        return jnp.einsum('thk,thkd->td', a.astype(x.dtype), vg)

    out = lax.map(block, xb).reshape(nb * tb, shp[-1])[:n_tok]
    return out.reshape(shp)


def setup_inputs(seed: int = 0) -> dict:
    key = jax.random.key(seed)
    k = jax.random.split(key, 26)
    f32 = jnp.float32

    def nrm(kk, shape, scale=1.0):
        return jax.random.normal(kk, shape, f32) * scale

    n_pages = PAST_LEN // PAGE_SIZE
    n_used = DEC_BATCH * n_pages
    n_pool = n_used + (n_used + 3) // 4
    page_table = jax.random.permutation(k[0], n_pool)[:n_used].reshape(DEC_BATCH, n_pages).astype(jnp.int32)
    even_col_scale = jnp.concatenate([jnp.full((c,), s, f32) for c, s in
                                      zip(EVEN_COLS, (1.0, 1.0, DN_BETA, 1.0, 1.0, DN_BETA))])
    odd_col_scale = jnp.concatenate([jnp.ones((2 * MOBA_W,), f32), jnp.full((MOBA_W,), DN_BETA, f32)])
    return {
        'x_prompt': nrm(k[1], (BATCH, SEQ, D_MODEL)),
        'x_sample': nrm(k[2], (DEC_BATCH, DEC_SEQ, D_MODEL)),
        'cache_diff_k': nrm(k[3], (N_EVEN, n_pool, PAGE_SIZE, DIFF_HEADS, 2, DIFF_DK)),
        'cache_diff_v': nrm(k[4], (N_EVEN, n_pool, PAGE_SIZE, DIFF_HEADS, DIFF_DV), DN_BETA),
        'cache_sb_k': nrm(k[5], (N_EVEN, n_pool, PAGE_SIZE, SB_HEADS, SB_DH)),
        'cache_sb_v': nrm(k[6], (N_EVEN, n_pool, PAGE_SIZE, SB_HEADS, SB_DH), DN_BETA),
        'cache_moba_k': nrm(k[7], (N_ODD, n_pool, PAGE_SIZE, MOBA_HEADS, MOBA_DH)),
        'cache_moba_v': nrm(k[8], (N_ODD, n_pool, PAGE_SIZE, MOBA_HEADS, MOBA_DH), DN_BETA),
        'page_table': page_table,
        'even_w_in': nrm(k[9], (N_EVEN, D_MODEL, sum(EVEN_COLS)), D_MODEL ** -0.5) * even_col_scale,
        'even_w_out': nrm(k[10], (N_EVEN, EVEN_MIX_W, D_MODEL), DN_BETA * EVEN_MIX_W ** -0.5),
        'diff_lambda_q1': nrm(k[11], (N_EVEN, DIFF_DK), 0.1),
        'diff_lambda_k1': nrm(k[12], (N_EVEN, DIFF_DK), 0.1),
        'diff_lambda_q2': nrm(k[13], (N_EVEN, DIFF_DK), 0.1),
        'diff_lambda_k2': nrm(k[14], (N_EVEN, DIFF_DK), 0.1),
        'diff_subln_g': 1.0 + nrm(k[15], (N_EVEN, DIFF_DV), 0.02),
        'odd_w_in': nrm(k[16], (N_ODD, D_MODEL, 3 * MOBA_W), D_MODEL ** -0.5) * odd_col_scale,
        'odd_w_out': nrm(k[17], (N_ODD, MOBA_W, D_MODEL), DN_BETA * MOBA_W ** -0.5),
        'ln_mix_g': 1.0 + nrm(k[18], (DEPTH, D_MODEL), 0.02),
        'ln_mix_b': nrm(k[19], (DEPTH, D_MODEL), 0.02),
        'ln_ffn_g': 1.0 + nrm(k[20], (DEPTH, D_MODEL), 0.02),
        'ln_ffn_b': nrm(k[21], (DEPTH, D_MODEL), 0.02),
        'peer_w_q': nrm(k[22], (DEPTH, D_MODEL, PEER_HEADS * PEER_DQ), D_MODEL ** -0.5),
        'peer_subkeys': nrm(k[23], (DEPTH, 2, PEER_NKEYS, PEER_DQ // 2), (PEER_DQ // 2) ** -0.5),
        'peer_u': nrm(k[24], (DEPTH, PEER_N, D_MODEL), D_MODEL ** -0.5),
        'peer_v': nrm(k[25], (DEPTH, PEER_N, D_MODEL), DN_BETA * PEER_HEADS ** -0.5),
    }


def reference(x_prompt, x_sample, cache_diff_k, cache_diff_v, cache_sb_k, cache_sb_v,
              cache_moba_k, cache_moba_v, page_table,
              even_w_in, even_w_out, diff_lambda_q1, diff_lambda_k1, diff_lambda_q2,
              diff_lambda_k2, diff_subln_g, odd_w_in, odd_w_out,
              ln_mix_g, ln_mix_b, ln_ffn_g, ln_ffn_b,
              peer_w_q, peer_subkeys, peer_u, peer_v):
    past_len = page_table.shape[1] * cache_diff_k.shape[2]
    pos_p = jnp.arange(x_prompt.shape[1], dtype=jnp.int32)
    pos_s = past_len + jnp.arange(x_sample.shape[1], dtype=jnp.int32)
    hp, hs = x_prompt, x_sample
    p_dk, p_dv, p_sk, p_sv, p_mk, p_mv = [], [], [], [], [], []
    s_dk, s_dv, s_sk, s_sv, s_mk, s_mv = [], [], [], [], [], []

    for layer in range(DEPTH):
        li = layer // 2
        if layer % 2 == 0:
            lam_init = 0.8 - 0.6 * math.exp(-0.3 * layer)
            lam = (jnp.exp(jnp.sum(diff_lambda_q1[li] * diff_lambda_k1[li]).astype(jnp.float32))
                   - jnp.exp(jnp.sum(diff_lambda_q2[li] * diff_lambda_k2[li]).astype(jnp.float32))
                   + lam_init)
            mix_p, (kd, vd, ks, vs) = _even_prompt(hp, pos_p, even_w_in[li], even_w_out[li],
                                                   lam, lam_init, diff_subln_g[li])
            p_dk.append(kd); p_dv.append(vd); p_sk.append(ks); p_sv.append(vs)
            mix_s, (kd, vd, ks, vs) = _even_sample(hs, pos_s, page_table, cache_diff_k, cache_diff_v,
                                                   cache_sb_k, cache_sb_v, li, even_w_in[li],
                                                   even_w_out[li], lam, lam_init, diff_subln_g[li])
            s_dk.append(kd); s_dv.append(vd); s_sk.append(ks); s_sv.append(vs)
        else:
            mix_p, (km, vm) = _odd_prompt(hp, pos_p, odd_w_in[li], odd_w_out[li])
            p_mk.append(km); p_mv.append(vm)
            mix_s, (km, vm) = _odd_sample(hs, pos_s, page_table, cache_moba_k, cache_moba_v, li,
                                          odd_w_in[li], odd_w_out[li])
            s_mk.append(km); s_mv.append(vm)
        hp = _layernorm(DN_ALPHA * hp + mix_p, ln_mix_g[layer], ln_mix_b[layer])
        hs = _layernorm(DN_ALPHA * hs + mix_s, ln_mix_g[layer], ln_mix_b[layer])
        u_l, v_l = peer_u[layer], peer_v[layer]
        hp = _layernorm(DN_ALPHA * hp + _peer(hp, peer_w_q[layer], peer_subkeys[layer], u_l, v_l),
                        ln_ffn_g[layer], ln_ffn_b[layer])
        hs = _layernorm(DN_ALPHA * hs + _peer(hs, peer_w_q[layer], peer_subkeys[layer], u_l, v_l),
                        ln_ffn_g[layer], ln_ffn_b[layer])

    return (hp, hs,
            jnp.stack(p_dk), jnp.stack(p_dv), jnp.stack(p_sk), jnp.stack(p_sv),
            jnp.stack(p_mk), jnp.stack(p_mv),
            jnp.stack(s_dk), jnp.stack(s_dv), jnp.stack(s_sk), jnp.stack(s_sv),
            jnp.stack(s_mk), jnp.stack(s_mv))
```

```python
import functools
import math

import jax
import jax.numpy as jnp
from jax import lax
from jax.experimental import pallas as pl
from jax.experimental.pallas import tpu as pltpu

F32 = jnp.float32
BF16 = jnp.bfloat16
I32 = jnp.int32

LANES = 128
DIFF_HEADS = 8
DIFF_DK = 64
SB_HEADS = 8
MOBA_HEADS = 16
HEAD_W = 128
MOBA_BLOCK = 256
MOBA_TOPK = 3
ROPE_THETA = 10000.0
PEER_HEADS = 8
PEER_NKEYS = 128
PEER_TOPK = 16
DEPTH = 2
DN_ALPHA = (2 * DEPTH) ** 0.25
LN_EPS = 1e-5
SUBLN_EPS = 1e-5
NEG_INF = float("-inf")

TOKEN_TILE = 256
ATTN_TILE = 256
PEER_CHUNK = 1024
VMEM_LIMIT = 56 * 1024 * 1024

_NT = (((1,), (1,)), ((), ()))


def _params(sem):
    return pltpu.CompilerParams(dimension_semantics=sem, vmem_limit_bytes=VMEM_LIMIT)


def _dot(a, b):
    return jnp.dot(a, b, preferred_element_type=F32)


def _dot_nt(a, b):
    return lax.dot_general(a, b, _NT, preferred_element_type=F32)


def _split_bf16(x):
    hi = x.astype(BF16)
    lo = (x - hi.astype(F32)).astype(BF16)
    return hi, lo


def _proj_body(x_ref, w_ref, cos_ref, sin_ref, *out_refs, rope, want_mean):
    of_ref, ob_ref = out_refs[0], out_refs[1]
    n = w_ref.shape[1]
    x = x_ref[...]
    for c in range(n // 256):
        acc = _dot(x, w_ref[:, c * 256:(c + 1) * 256])
        if rope:
            parts = []
            for hh in range(2):
                y = acc[:, hh * LANES:(hh + 1) * LANES]
                if rope == LANES:
                    sw = pltpu.roll(y, LANES // 2, 1)
                else:
                    lane = lax.broadcasted_iota(I32, y.shape, 1)
                    first = (lane % rope) < rope // 2
                    sw = jnp.where(first, pltpu.roll(y, LANES - rope // 2, 1),
                                   pltpu.roll(y, rope // 2, 1))
                parts.append(y * cos_ref[...] + sw * sin_ref[...])
            acc = jnp.concatenate(parts, axis=1)
        of_ref[:, c * 256:(c + 1) * 256] = acc
        ob_ref[:, c * 256:(c + 1) * 256] = acc.astype(BF16)
        if want_mean:
            out_refs[2][0, :, c * 256:(c + 1) * 256] = jnp.mean(acc, axis=0, keepdims=True)


def _proj(xb, wb, cos, sin, rope=0, want_mean=False):
    m, k = xb.shape
    n = wb.shape[1]
    tm = TOKEN_TILE
    out_shape = [jax.ShapeDtypeStruct((m, n), F32), jax.ShapeDtypeStruct((m, n), BF16)]
    out_specs = [pl.BlockSpec((tm, n), lambda i: (i, 0)), pl.BlockSpec((tm, n), lambda i: (i, 0))]
    if want_mean:
        out_shape.append(jax.ShapeDtypeStruct((m // tm, 1, n), F32))
        out_specs.append(pl.BlockSpec((1, 1, n), lambda i: (i, 0, 0)))
    return pl.pallas_call(
        functools.partial(_proj_body, rope=rope, want_mean=want_mean),
        grid=(m // tm,),
        in_specs=[pl.BlockSpec((tm, k), lambda i: (i, 0)),
                  pl.BlockSpec((k, n), lambda i: (0, 0)),
                  pl.BlockSpec((tm, LANES), lambda i: (i, 0)),
                  pl.BlockSpec((tm, LANES), lambda i: (i, 0))],
        out_specs=out_specs,
        out_shape=out_shape,
        compiler_params=_params(("parallel",)),
        name="proj",
    )(xb, wb, cos, sin)


def _layernorm_rows(y, g, b):
    mu = jnp.mean(y, axis=-1, keepdims=True)
    d = y - mu
    var = jnp.mean(d * d, axis=-1, keepdims=True)
    return d * lax.rsqrt(var + LN_EPS) * g + b


def _outproj_ln_body(*refs, n_in):
    a_refs = refs[:n_in]
    w_ref, r_ref, g_ref, b_ref, of_ref, ob_ref = refs[n_in:]
    y = DN_ALPHA * r_ref[...]
    off = 0
    for a_ref in a_refs:
        kk = a_ref.shape[1]
        y = y + _dot(a_ref[...].astype(BF16), w_ref[off:off + kk, :])
        off += kk
    o = _layernorm_rows(y, g_ref[...], b_ref[...])
    of_ref[...] = o
    ob_ref[...] = o.astype(BF16)


def _outproj_ln(a_list, wb, resid, g, b):
    m, d = resid.shape
    tm = TOKEN_TILE
    in_specs = [pl.BlockSpec((tm, a.shape[1]), lambda i: (i, 0)) for a in a_list]
    in_specs += [pl.BlockSpec(wb.shape, lambda i: (0, 0)),
                 pl.BlockSpec((tm, d), lambda i: (i, 0)),
                 pl.BlockSpec((1, d), lambda i: (0, 0)),
                 pl.BlockSpec((1, d), lambda i: (0, 0))]
    return pl.pallas_call(
        functools.partial(_outproj_ln_body, n_in=len(a_list)),
        grid=(m // tm,),
        in_specs=in_specs,
        out_specs=[pl.BlockSpec((tm, d), lambda i: (i, 0)), pl.BlockSpec((tm, d), lambda i: (i, 0))],
        out_shape=[jax.ShapeDtypeStruct((m, d), F32), jax.ShapeDtypeStruct((m, d), BF16)],
        compiler_params=_params(("parallel",)),
        name="outproj_ln",
    )(*a_list, wb, resid, g.reshape(1, d), b.reshape(1, d))


def _softmax_update(s, v, m_ref, l_ref, acc_ref):
    m_prev = m_ref[...]
    m_new = jnp.maximum(m_prev, jnp.max(s, axis=-1, keepdims=True))
    alpha = jnp.exp(m_prev - m_new)
    p = jnp.exp(s - m_new)
    l_ref[...] = alpha * l_ref[...] + jnp.sum(p, axis=-1, keepdims=True)
    acc_ref[...] = alpha * acc_ref[...] + _dot(p.astype(BF16), v)
    m_ref[...] = m_new


def _stack_diff_q(q):
    lane = lax.broadcasted_iota(I32, q.shape, 1)
    zero = jnp.zeros_like(q)
    return jnp.concatenate([jnp.where(lane < DIFF_DK, q, zero),
                            jnp.where(lane >= DIFF_DK, q, zero)], axis=0)


def _diff_finish(acc, l, rows, lam_ref, g_ref, lam_init):
    o0 = acc[:rows] / l[:rows]
    o1 = acc[rows:] / l[rows:]
    lam = (jnp.exp(jnp.sum(lam_ref[0:1, :] * lam_ref[1:2, :], axis=-1, keepdims=True))
           - jnp.exp(jnp.sum(lam_ref[2:3, :] * lam_ref[3:4, :], axis=-1, keepdims=True))
           + lam_init)
    o = o0 - lam * o1
    o = o * lax.rsqrt(jnp.mean(o * o, axis=-1, keepdims=True) + SUBLN_EPS)
    return o * g_ref[...] * (1.0 - lam_init)


def _strict_upper_ones(n):
    r = lax.broadcasted_iota(I32, (n, n), 0)
    c = lax.broadcasted_iota(I32, (n, n), 1)
    return jnp.where(r > c, 1.0, 0.0).astype(BF16)


def _stick_break_weights(z, tri, past, c_ref):
    t = jnp.log(1.0 + jnp.exp(-jnp.abs(z)))
    log_beta = jnp.minimum(z, 0.0) - t
    log_keep = -jnp.maximum(z, 0.0) - t
    if past is not None:
        log_keep = jnp.where(past, log_keep, 0.0)
    hi, lo = _split_bf16(log_keep)
    rows = z.shape[0]
    parts = _dot(jnp.concatenate([hi, lo], axis=0), tri)
    between = parts[:rows] + parts[rows:] + c_ref[...]
    a = jnp.exp(log_beta + between)
    if past is not None:
        a = jnp.where(past, a, 0.0)
    c_ref[...] += jnp.sum(log_keep, axis=-1, keepdims=True)
    return a.astype(BF16)


def _diff_prompt_body(lam_ref, g_ref, q_ref, k_ref, v_ref, o_ref, acc_ref, m_ref, l_ref, *, lam_init):
    tq = q_ref.shape[0]
    i = pl.program_id(2)
    qz = _stack_diff_q(q_ref[...])
    m_ref[...] = jnp.full(m_ref.shape, NEG_INF, F32)
    l_ref[...] = jnp.zeros(l_ref.shape, F32)
    acc_ref[...] = jnp.zeros(acc_ref.shape, F32)
    scale = DIFF_DK ** -0.5

    def step(j, diagonal):
        rows = pl.ds(pl.multiple_of(j * tq, tq), tq)
        s = _dot_nt(qz, k_ref[rows, :]) * scale
        if diagonal:
            r = lax.broadcasted_iota(I32, s.shape, 0) % tq
            c = lax.broadcasted_iota(I32, s.shape, 1)
            s = jnp.where(c <= r, s, NEG_INF)
        _softmax_update(s, v_ref[rows, :], m_ref, l_ref, acc_ref)

    def body(j, carry):
        step(j, False)
        return carry

    lax.fori_loop(0, i, body, 0)
    step(i, True)
    o = _diff_finish(acc_ref[...], l_ref[...], tq, lam_ref, g_ref, lam_init)
    o_ref[...] = o.astype(BF16)


def _diff_prompt(qb, kb, vb, lam4, g, batch, seq, lam_init):
    tq = ATTN_TILE
    nq = seq // tq
    return pl.pallas_call(
        functools.partial(_diff_prompt_body, lam_init=lam_init),
        grid=(batch, DIFF_HEADS, nq),
        in_specs=[pl.BlockSpec((4, DIFF_DK), lambda b, h, i: (0, 0)),
                  pl.BlockSpec((1, HEAD_W), lambda b, h, i: (0, 0)),
                  pl.BlockSpec((tq, HEAD_W), lambda b, h, i: (b * nq + i, h)),
                  pl.BlockSpec((seq, HEAD_W), lambda b, h, i: (b, h)),
                  pl.BlockSpec((seq, HEAD_W), lambda b, h, i: (b, h))],
        out_specs=pl.BlockSpec((tq, HEAD_W), lambda b, h, i: (b * nq + i, h)),
        out_shape=jax.ShapeDtypeStruct(qb.shape, BF16),
        scratch_shapes=[pltpu.VMEM((2 * tq, HEAD_W), F32),
                        pltpu.VMEM((2 * tq, 1), F32),
                        pltpu.VMEM((2 * tq, 1), F32)],
        compiler_params=_params(("parallel", "parallel", "arbitrary")),
        name="diff_prompt",
    )(lam4, g, qb, kb, vb)


def _sb_prompt_body(q_ref, k_ref, v_ref, o_ref, acc_ref, c_ref):
    tq = q_ref.shape[0]
    i = pl.program_id(2)
    q = q_ref[...]
    tri = _strict_upper_ones(tq)
    acc_ref[...] = jnp.zeros(acc_ref.shape, F32)
    c_ref[...] = jnp.zeros(c_ref.shape, F32)
    scale = HEAD_W ** -0.5
    r = lax.broadcasted_iota(I32, (tq, tq), 0)
    c = lax.broadcasted_iota(I32, (tq, tq), 1)

    def step(j, diagonal):
        rows = pl.ds(pl.multiple_of(j * tq, tq), tq)
        z = _dot_nt(q, k_ref[rows, :]) * scale
        a = _stick_break_weights(z, tri, (c < r) if diagonal else None, c_ref)
        acc_ref[...] += _dot(a, v_ref[rows, :])

    step(i, True)

    def body(jj, carry):
        step(i - 1 - jj, False)
        return carry

    lax.fori_loop(0, i, body, 0)
    o_ref[...] = acc_ref[...].astype(BF16)


def _sb_prompt(qb, kb, vb, batch, seq):
    tq = ATTN_TILE
    nq = seq // tq
    return pl.pallas_call(
        _sb_prompt_body,
        grid=(batch, SB_HEADS, nq),
        in_specs=[pl.BlockSpec((tq, HEAD_W), lambda b, h, i: (b * nq + i, h)),
                  pl.BlockSpec((seq, HEAD_W), lambda b, h, i: (b, h)),
                  pl.BlockSpec((seq, HEAD_W), lambda b, h, i: (b, h))],
        out_specs=pl.BlockSpec((tq, HEAD_W), lambda b, h, i: (b * nq + i, h)),
        out_shape=jax.ShapeDtypeStruct(qb.shape, BF16),
        scratch_shapes=[pltpu.VMEM((tq, HEAD_W), F32), pltpu.VMEM((tq, 1), F32)],
        compiler_params=_params(("parallel", "parallel", "arbitrary")),
        name="sb_prompt",
    )(qb, kb, vb)


def _moba_prompt_body(qf_ref, qb_ref, km_ref, k_ref, v_ref, o_ref, acc_ref, m_ref, l_ref):
    tq = qb_ref.shape[0]
    nb = km_ref.shape[0]
    i = pl.program_id(2)
    qb = qb_ref[...]
    scale = HEAD_W ** -0.5

    qh, ql = _split_bf16(qf_ref[...])
    kh, kl = _split_bf16(km_ref[...])
    gate = _dot_nt(qh, kh) + _dot_nt(qh, kl) + _dot_nt(ql, kh)
    n_io = lax.broadcasted_iota(I32, gate.shape, 1)
    gm = jnp.where(n_io < i, gate, NEG_INF)
    rank = jnp.zeros(gate.shape, I32)
    for n2 in range(nb):
        col = gm[:, n2:n2 + 1]
        beats = jnp.where(col > gm, 1, jnp.where(col == gm, jnp.where(n2 < n_io, 1, 0), 0))
        rank = rank + beats
    sel = jnp.where(n_io < i, jnp.where(rank < MOBA_TOPK, 1.0, 0.0), 0.0)

    m_ref[...] = jnp.full(m_ref.shape, NEG_INF, F32)
    l_ref[...] = jnp.zeros(l_ref.shape, F32)
    acc_ref[...] = jnp.zeros(acc_ref.shape, F32)

    own = pl.ds(pl.multiple_of(i * tq, tq), tq)
    s = _dot_nt(qb, k_ref[own, :]) * scale
    r = lax.broadcasted_iota(I32, s.shape, 0)
    c = lax.broadcasted_iota(I32, s.shape, 1)
    _softmax_update(jnp.where(c <= r, s, NEG_INF), v_ref[own, :], m_ref, l_ref, acc_ref)

    def body(n, carry):
        rows = pl.ds(pl.multiple_of(n * tq, tq), tq)
        chosen = jnp.sum(jnp.where(n_io == n, sel, 0.0), axis=-1, keepdims=True)
        s = _dot_nt(qb, k_ref[rows, :]) * scale
        _softmax_update(jnp.where(chosen > 0.0, s, NEG_INF), v_ref[rows, :], m_ref, l_ref, acc_ref)
        return carry

    lax.fori_loop(0, i, body, 0)
    o_ref[...] = (acc_ref[...] / l_ref[...]).astype(BF16)


def _moba_prompt(qf, qb, kmean, kb, vb, batch, seq):
    tq = MOBA_BLOCK
    nq = seq // tq
    return pl.pallas_call(
        _moba_prompt_body,
        grid=(batch, MOBA_HEADS, nq),
        in_specs=[pl.BlockSpec((tq, HEAD_W), lambda b, h, i: (b * nq + i, h)),
                  pl.BlockSpec((tq, HEAD_W), lambda b, h, i: (b * nq + i, h)),
                  pl.BlockSpec((None, nq, HEAD_W), lambda b, h, i: (b, 0, h)),
                  pl.BlockSpec((seq, HEAD_W), lambda b, h, i: (b, h)),
                  pl.BlockSpec((seq, HEAD_W), lambda b, h, i: (b, h))],
        out_specs=pl.BlockSpec((tq, HEAD_W), lambda b, h, i: (b * nq + i, h)),
        out_shape=jax.ShapeDtypeStruct(qb.shape, BF16),
        scratch_shapes=[pltpu.VMEM((tq, HEAD_W), F32),
                        pltpu.VMEM((tq, 1), F32),
                        pltpu.VMEM((tq, 1), F32)],
        compiler_params=_params(("parallel", "parallel", "arbitrary")),
        name="moba_prompt",
    )(qf, qb, kmean, kb, vb)


def _pad_rows(x, n):
    return jnp.concatenate([x, jnp.zeros((n - x.shape[0],) + x.shape[1:], x.dtype)], axis=0)


def _same_head(shape, heads):
    rr = lax.broadcasted_iota(I32, shape, 0)
    cc = lax.broadcasted_iota(I32, shape, 1)
    return (rr & (heads - 1)) == (cc & (heads - 1)), rr, cc


def _page_matrix(ref):
    rows, heads, width = ref.shape
    return ref[...].reshape(rows * heads, width).astype(BF16)


def _diff_sample_body(pt_ref, lam_ref, g_ref, q_ref, kn_ref, vn_ref, kp_ref, vp_ref, o_ref,
                      acc_ref, m_ref, l_ref, *, lam_init):
    p = pl.program_id(1)
    page, heads, _ = kp_ref.shape
    n_rows = q_ref.shape[0]
    shift = heads.bit_length() - 1
    scale = DIFF_DK ** -0.5
    qz = _stack_diff_q(q_ref[...]).astype(BF16)

    @pl.when(p == 0)
    def _():
        m_ref[...] = jnp.full(m_ref.shape, NEG_INF, F32)
        l_ref[...] = jnp.zeros(l_ref.shape, F32)
        acc_ref[...] = jnp.zeros(acc_ref.shape, F32)

    def update(k2d, v2d, causal):
        s = _dot_nt(qz, k2d) * scale
        ok, rr, cc = _same_head(s.shape, heads)
        s = jnp.where(ok, s, NEG_INF)
        if causal:
            query = (rr & (n_rows - 1)) >> shift
            s = jnp.where((cc >> shift) <= query, s, NEG_INF)
        _softmax_update(s, v2d, m_ref, l_ref, acc_ref)

    update(_page_matrix(kp_ref), _page_matrix(vp_ref), False)

    @pl.when(p == pl.num_programs(1) - 1)
    def _():
        update(_pad_rows(kn_ref[...], 2 * n_rows).astype(BF16),
               _pad_rows(vn_ref[...], 2 * n_rows).astype(BF16), True)
        o_ref[...] = _diff_finish(acc_ref[...], l_ref[...], n_rows, lam_ref, g_ref, lam_init)


def _sample_specs(n_rows, page, heads, n_steps, reverse=False):
    row = pl.BlockSpec((n_rows, HEAD_W), lambda s, p, pt: (s, 0))
    if reverse:
        pg = pl.BlockSpec((None, page, heads, HEAD_W), lambda s, p, pt: (pt[s, n_steps - 1 - p], 0, 0, 0))
    else:
        pg = pl.BlockSpec((None, page, heads, HEAD_W), lambda s, p, pt: (pt[s, p], 0, 0, 0))
    return row, pg


def _diff_sample(page_table, lam4, g, q2, k2, v2, cache_k, cache_v, nq, lam_init):
    n_seq, n_pages = page_table.shape
    _, page, heads, _ = cache_k.shape
    n_rows = nq * heads
    assert n_rows & (n_rows - 1) == 0 and heads & (heads - 1) == 0
    row, pg = _sample_specs(n_rows, page, heads, n_pages)
    return pl.pallas_call(
        functools.partial(_diff_sample_body, lam_init=lam_init),
        grid_spec=pltpu.PrefetchScalarGridSpec(
            num_scalar_prefetch=1,
            grid=(n_seq, n_pages),
            in_specs=[pl.BlockSpec((4, DIFF_DK), lambda s, p, pt: (0, 0)),
                      pl.BlockSpec((1, HEAD_W), lambda s, p, pt: (0, 0)),
                      row, row, row, pg, pg],
            out_specs=row,
            scratch_shapes=[pltpu.VMEM((2 * n_rows, HEAD_W), F32),
                            pltpu.VMEM((2 * n_rows, 1), F32),
                            pltpu.VMEM((2 * n_rows, 1), F32)]),
        out_shape=jax.ShapeDtypeStruct(q2.shape, F32),
        compiler_params=_params(("parallel", "arbitrary")),
        name="diff_sample",
    )(page_table, lam4, g, q2, k2, v2, cache_k, cache_v)


def _sb_sample_body(pt_ref, q_ref, kn_ref, vn_ref, kp_ref, vp_ref, o_ref, acc_ref, c_ref):
    p = pl.program_id(1)
    page, heads, _ = kp_ref.shape
    nq = q_ref.shape[0] // heads
    pad = 16
    scale = HEAD_W ** -0.5
    tri = _strict_upper_ones(page)

    def update(get_k, get_v, past):
        zs = []
        for h in range(heads):
            q = _pad_rows(q_ref[h * nq:(h + 1) * nq, :], pad).astype(BF16)
            zs.append(_dot_nt(q, get_k(h).astype(BF16)) * scale)
        a = _stick_break_weights(jnp.concatenate(zs, axis=0), tri, past, c_ref)
        for h in range(heads):
            rows = slice(h * pad, (h + 1) * pad)
            acc_ref[rows, :] += _dot(a[rows], get_v(h).astype(BF16))

    @pl.when(p == 0)
    def _():
        acc_ref[...] = jnp.zeros(acc_ref.shape, F32)
        c_ref[...] = jnp.zeros(c_ref.shape, F32)
        r = lax.broadcasted_iota(I32, (heads * pad, page), 0) & (pad - 1)
        c = lax.broadcasted_iota(I32, (heads * pad, page), 1)
        update(lambda h: _pad_rows(kn_ref[h * nq:(h + 1) * nq, :], page),
               lambda h: _pad_rows(vn_ref[h * nq:(h + 1) * nq, :], page), c < r)

    update(lambda h: kp_ref[:, h, :], lambda h: vp_ref[:, h, :], None)

    @pl.when(p == pl.num_programs(1) - 1)
    def _():
        for h in range(heads):
            o_ref[h * nq:(h + 1) * nq, :] = acc_ref[h * pad:h * pad + nq, :]


def _sb_sample(page_table, q2, k2, v2, cache_k, cache_v, nq):
    n_seq, n_pages = page_table.shape
    _, page, heads, _ = cache_k.shape
    row, pg = _sample_specs(nq * heads, page, heads, n_pages, reverse=True)
    return pl.pallas_call(
        _sb_sample_body,
        grid_spec=pltpu.PrefetchScalarGridSpec(
            num_scalar_prefetch=1,
            grid=(n_seq, n_pages),
            in_specs=[row, row, row, pg, pg],
            out_specs=row,
            scratch_shapes=[pltpu.VMEM((heads * 16, HEAD_W), F32),
                            pltpu.VMEM((heads * 16, 1), F32)]),
        out_shape=jax.ShapeDtypeStruct(q2.shape, F32),
        compiler_params=_params(("parallel", "arbitrary")),
        name="sb_sample",
    )(page_table, q2, k2, v2, cache_k, cache_v)


def _moba_sample_body(pt_ref, q_ref, kn_ref, vn_ref, ka_ref, kb_ref, va_ref, vb_ref, o_ref,
                      o_sc, m_sc, l_sc, g_sc):
    n = pl.program_id(1)
    page, heads, _ = ka_ref.shape
    n_rows = q_ref.shape[0]
    nb = g_sc.shape[0]
    shift = heads.bit_length() - 1
    scale = HEAD_W ** -0.5
    q = q_ref[...]
    qb = q.astype(BF16)

    k3 = jnp.concatenate([ka_ref[...], kb_ref[...]], axis=0)
    v3 = jnp.concatenate([va_ref[...], vb_ref[...]], axis=0)
    k_mean = jnp.mean(k3, axis=0)
    g_sc[n] = jnp.sum(q * jnp.concatenate([k_mean] * (n_rows // heads), axis=0), axis=-1, keepdims=True)
    k2d = k3.reshape(2 * page * heads, HEAD_W).astype(BF16)
    v2d = v3.reshape(2 * page * heads, HEAD_W).astype(BF16)
    s = _dot_nt(qb, k2d) * scale
    ok, _, _ = _same_head(s.shape, heads)
    s = jnp.where(ok, s, NEG_INF)
    m = jnp.max(s, axis=-1, keepdims=True)
    pr = jnp.exp(s - m)
    m_sc[n] = m
    l_sc[n] = jnp.sum(pr, axis=-1, keepdims=True)
    o_sc[n] = _dot(pr.astype(BF16), v2d)

    @pl.when(n == nb - 1)
    def _():
        n_io = lax.broadcasted_iota(I32, (nb, n_rows, 1), 0)
        gm = g_sc[...]
        sel = jnp.zeros(gm.shape, F32)
        for _ in range(MOBA_TOPK):
            mx = jnp.max(gm, axis=0, keepdims=True)
            idx = jnp.min(jnp.where(gm == mx, n_io, nb), axis=0, keepdims=True)
            hit = n_io == idx
            sel = jnp.where(hit, jnp.where(mx > NEG_INF, 1.0, 0.0), sel)
            gm = jnp.where(hit, NEG_INF, gm)
        s_own = _dot_nt(qb, kn_ref[...].astype(BF16)) * scale
        ok_own, rr, cc = _same_head(s_own.shape, heads)
        s_own = jnp.where(ok_own, jnp.where((cc >> shift) <= (rr >> shift), s_own, NEG_INF), NEG_INF)
        m_own = jnp.max(s_own, axis=-1, keepdims=True)
        pr_own = jnp.exp(s_own - m_own)
        l_own = jnp.sum(pr_own, axis=-1, keepdims=True)
        o_own = _dot(pr_own.astype(BF16), vn_ref[...].astype(BF16))
        m_all = m_sc[...]
        m_tot = jnp.maximum(m_own, jnp.max(jnp.where(sel > 0.0, m_all, NEG_INF), axis=0))
        w = jnp.where(sel > 0.0, jnp.exp(m_all - m_tot), 0.0)
        w_own = jnp.exp(m_own - m_tot)
        l_tot = w_own * l_own + jnp.sum(w * l_sc[...], axis=0)
        o_tot = w_own * o_own + jnp.sum(w * o_sc[...], axis=0)
        o_ref[...] = o_tot / l_tot


def _moba_sample(page_table, q2, k2, v2, cache_k, cache_v, nq):
    n_seq, n_pages = page_table.shape
    _, page, heads, _ = cache_k.shape
    n_rows = nq * heads
    assert MOBA_BLOCK == 2 * page and n_pages % 2 == 0 and heads & (heads - 1) == 0
    nb = n_pages // 2
    row = pl.BlockSpec((n_rows, HEAD_W), lambda s, n, pt: (s, 0))
    pg_a = pl.BlockSpec((None, page, heads, HEAD_W), lambda s, n, pt: (pt[s, 2 * n], 0, 0, 0))
    pg_b = pl.BlockSpec((None, page, heads, HEAD_W), lambda s, n, pt: (pt[s, 2 * n + 1], 0, 0, 0))
    return pl.pallas_call(
        _moba_sample_body,
        grid_spec=pltpu.PrefetchScalarGridSpec(
            num_scalar_prefetch=1,
            grid=(n_seq, nb),
            in_specs=[row, row, row, pg_a, pg_b, pg_a, pg_b],
            out_specs=row,
            scratch_shapes=[pltpu.VMEM((nb, n_rows, HEAD_W), F32),
                            pltpu.VMEM((nb, n_rows, 1), F32),
                            pltpu.VMEM((nb, n_rows, 1), F32),
                            pltpu.VMEM((nb, n_rows, 1), F32)]),
        out_shape=jax.ShapeDtypeStruct(q2.shape, F32),
        compiler_params=_params(("parallel", "arbitrary")),
        name="moba_sample",
    )(page_table, q2, k2, v2, cache_k, cache_k, cache_v, cache_v)


def _topk_rows(s, k_top):
    n = s.shape[0]
    row = lax.broadcasted_iota(I32, s.shape, 0)
    rank = jnp.full(s.shape, k_top, I32)
    vals, idxs = [], []
    for k in range(k_top):
        mx = jnp.max(s, axis=0, keepdims=True)
        idx = jnp.min(jnp.where(s == mx, row, n), axis=0, keepdims=True)
        hit = row == idx
        rank = jnp.where(hit, k, rank)
        s = jnp.where(hit, NEG_INF, s)
        vals.append(mx)
        idxs.append(idx)
    return jnp.concatenate(vals, axis=0), jnp.concatenate(idxs, axis=0), rank


def _peer_route_body(x_ref, wq_ref, sk_ref, e0_ref, thr_ref, e1_ref, r1_ref):
    kt = PEER_TOPK
    x = x_ref[...]
    k_io = lax.broadcasted_iota(I32, (kt, x.shape[0]), 0)
    for h in range(PEER_HEADS):
        scores, tops = [], []
        for half in range(2):
            col = (h * 2 + half) * PEER_NKEYS
            q = _dot(x, wq_ref[:, col:col + PEER_NKEYS])
            s = _dot_nt(sk_ref[half], q.astype(BF16))
            scores.append(s)
            tops.append(_topk_rows(s, kt))
        (sv0, _, rank0), (sv1, _, rank1) = tops
        cand = jnp.concatenate([sv0[a:a + 1] + sv1 for a in range(kt)], axis=0)
        _, ci, _ = _topk_rows(cand, kt)
        count = jnp.zeros(k_io.shape, F32)
        for j in range(kt):
            count = count + jnp.where(k_io == (ci[j:j + 1] >> (kt.bit_length() - 1)), 1.0, 0.0)
        w0 = jnp.exp(sv0 - sv0[0:1])
        w1 = jnp.exp(sv1 - sv1[0:1])
        z = jnp.zeros((1, x.shape[0]), F32)
        for a in range(kt):
            inner = jnp.sum(jnp.where(k_io.astype(F32) < count[a:a + 1], w1, 0.0), axis=0, keepdims=True)
            z = z + w0[a:a + 1] * inner
        thr = jnp.zeros(rank0.shape, F32)
        for a in range(kt):
            thr = jnp.where(rank0 == a, count[a:a + 1], thr)
        e0_ref[h] = jnp.where(rank0 < kt, jnp.exp(scores[0] - sv0[0:1]), 0.0) / z
        thr_ref[h] = thr
        e1_ref[h] = jnp.where(rank1 < kt, jnp.exp(scores[1] - sv1[0:1]), 0.0)
        r1_ref[h] = rank1.astype(F32)


def _peer_route(xb, wqb, skb):
    m, d = xb.shape
    tb = TOKEN_TILE
    spec = pl.BlockSpec((PEER_HEADS, PEER_NKEYS, tb), lambda i: (0, 0, i))
    shape = jax.ShapeDtypeStruct((PEER_HEADS, PEER_NKEYS, m), F32)
    return pl.pallas_call(
        _peer_route_body,
        grid=(m // tb,),
        in_specs=[pl.BlockSpec((tb, d), lambda i: (i, 0)),
                  pl.BlockSpec(wqb.shape, lambda i: (0, 0)),
                  pl.BlockSpec(skb.shape, lambda i: (0, 0, 0))],
        out_specs=[spec] * 4,
        out_shape=[shape] * 4,
        compiler_params=_params(("parallel",)),
        name="peer_route",
    )(xb, wqb, skb)


def _gelu(x):
    return 0.5 * x * (1.0 + lax.erf(x * (2.0 ** -0.5)))


def _peer_expert_body(x_ref, e0_ref, thr_ref, e1_ref, r1_ref, u_ref, vt_ref, r_ref, g_ref, b_ref,
                      of_ref, ob_ref, acc_ref):
    c = pl.program_id(1)
    rows_per_chunk = u_ref.shape[0] // PEER_NKEYS

    @pl.when(c == 0)
    def _():
        acc_ref[...] = jnp.zeros(acc_ref.shape, F32)

    x = x_ref[...]
    parts = []
    for rr in range(rows_per_chunk):
        i0 = c * rows_per_chunk + rr
        hid = _dot_nt(u_ref[rr * PEER_NKEYS:(rr + 1) * PEER_NKEYS, :], x)
        w = jnp.zeros(hid.shape, F32)
        for h in range(PEER_HEADS):
            keep = r1_ref[h] < thr_ref[h, pl.ds(i0, 1), :]
            w = w + jnp.where(keep, e1_ref[h], 0.0) * e0_ref[h, pl.ds(i0, 1), :]
        parts.append((_gelu(hid) * w).astype(BF16))
    aw = jnp.concatenate(parts, axis=0)
    acc_ref[...] += _dot(vt_ref[...], aw)

    @pl.when(c == pl.num_programs(1) - 1)
    def _():
        y = DN_ALPHA * r_ref[...] + acc_ref[...].T
        o = _layernorm_rows(y, g_ref[...], b_ref[...])
        of_ref[...] = o
        ob_ref[...] = o.astype(BF16)


def _peer_expert(xb, routes, ub, vtb, resid, g, b):
    m, d = xb.shape
    n_exp = ub.shape[0]
    tb = TOKEN_TILE
    ec = PEER_CHUNK
    rspec = pl.BlockSpec((PEER_HEADS, PEER_NKEYS, tb), lambda i, c: (0, 0, i))
    return pl.pallas_call(
        _peer_expert_body,
        grid=(m // tb, n_exp // ec),
        in_specs=[pl.BlockSpec((tb, d), lambda i, c: (i, 0)),
                  rspec, rspec, rspec, rspec,
                  pl.BlockSpec((ec, d), lambda i, c: (c, 0)),
                  pl.BlockSpec((d, ec), lambda i, c: (0, c)),
                  pl.BlockSpec((tb, d), lambda i, c: (i, 0)),
                  pl.BlockSpec((1, d), lambda i, c: (0, 0)),
                  pl.BlockSpec((1, d), lambda i, c: (0, 0))],
        out_specs=[pl.BlockSpec((tb, d), lambda i, c: (i, 0)), pl.BlockSpec((tb, d), lambda i, c: (i, 0))],
        out_shape=[jax.ShapeDtypeStruct((m, d), F32), jax.ShapeDtypeStruct((m, d), BF16)],
        scratch_shapes=[pltpu.VMEM((d, tb), F32)],
        compiler_params=_params(("parallel", "arbitrary")),
        name="peer_expert",
    )(xb, *routes, ub, vtb, resid, g.reshape(1, d), b.reshape(1, d))


def _peer_ln(h, hb, wqb, skb, ub, vtb, g, b):
    routes = _peer_route(hb, wqb, skb)
    return _peer_expert(hb, routes, ub, vtb, h, g, b)


def _token_head_rows(x):
    return x.reshape(-1, HEAD_W)


def _head_token_rows(x, n_seq):
    tokens = x.shape[0] // n_seq
    heads = x.shape[1] // HEAD_W
    return x.reshape(n_seq, tokens, heads, HEAD_W).transpose(0, 2, 1, 3).reshape(-1, HEAD_W)


def _rope_tables(pos, width):
    half = width // 2
    inv = ROPE_THETA ** (-jnp.arange(half, dtype=F32) / half)
    ang = pos.astype(F32)[:, None] * inv[None, :]
    cos = jnp.tile(jnp.cos(ang), (1, LANES // half))
    sin = jnp.sin(ang)
    sin = jnp.tile(jnp.concatenate([-sin, sin], axis=1), (1, LANES // width))
    return cos, sin


def kernel(x_prompt, x_sample, cache_diff_k, cache_diff_v, cache_sb_k, cache_sb_v, cache_moba_k,
           cache_moba_v, page_table, even_w_in, even_w_out, diff_lambda_q1, diff_lambda_k1,
           diff_lambda_q2, diff_lambda_k2, diff_subln_g, odd_w_in, odd_w_out, ln_mix_g, ln_mix_b,
           ln_ffn_g, ln_ffn_b, peer_w_q, peer_subkeys, peer_u, peer_v):
    batch, seq, d = x_prompt.shape
    n_seq, nq, _ = x_sample.shape
    page = cache_diff_k.shape[2]
    past_len = page_table.shape[1] * page
    depth = ln_mix_g.shape[0]
    assert depth == DEPTH and seq % ATTN_TILE == 0 and past_len % MOBA_BLOCK == 0
    assert (batch * seq) % TOKEN_TILE == 0 and (n_seq * nq) % TOKEN_TILE == 0 and nq == 8

    pos_p = jnp.tile(jnp.arange(seq, dtype=I32), batch)
    pos_s = jnp.tile(past_len + jnp.arange(nq, dtype=I32), n_seq)
    groups = [
        dict(h=x_prompt.reshape(batch * seq, d), pos=pos_p),
        dict(h=x_sample.reshape(n_seq * nq, d), pos=pos_s),
    ]
    for gr in groups:
        gr["hb"] = gr["h"].astype(BF16)
        gr["rope64"] = _rope_tables(gr["pos"], DIFF_DK)
        gr["rope128"] = _rope_tables(gr["pos"], HEAD_W)
    new_rows = [dict(), dict()]

    for layer in range(depth):
        li = layer // 2
        mix = []
        if layer % 2 == 0:
            lam_init = 0.8 - 0.6 * math.exp(-0.3 * layer)
            lam4 = jnp.stack([diff_lambda_q1[li], diff_lambda_k1[li], diff_lambda_q2[li], diff_lambda_k2[li]])
            sub_g = diff_subln_g[li].reshape(1, HEAD_W)
            w_in = even_w_in[li].astype(BF16)
            hw = DIFF_HEADS * HEAD_W
            w_cols = [w_in[:, j * hw:(j + 1) * hw] for j in range(6)]
            w_out = even_w_out[li].astype(BF16)
            ck = cache_diff_k[li].reshape(-1, page, DIFF_HEADS, HEAD_W)
            cv, sk, sv = cache_diff_v[li], cache_sb_k[li], cache_sb_v[li]
            for gi, gr in enumerate(groups):
                cos, sin = gr["rope64"]
                qd_f, qd_b = _proj(gr["hb"], w_cols[0], cos, sin, rope=DIFF_DK)
                kd_f, kd_b = _proj(gr["hb"], w_cols[1], cos, sin, rope=DIFF_DK)
                vd_f, vd_b = _proj(gr["hb"], w_cols[2], cos, sin)
                qs_f, qs_b = _proj(gr["hb"], w_cols[3], cos, sin)
                ks_f, ks_b = _proj(gr["hb"], w_cols[4], cos, sin)
                vs_f, vs_b = _proj(gr["hb"], w_cols[5], cos, sin)
                new_rows[gi].update(dk=kd_f, dv=vd_f, sk=ks_f, sv=vs_f)
                if gi == 0:
                    od = _diff_prompt(qd_b, kd_b, vd_b, lam4, sub_g, batch, seq, lam_init)
                    osb = _sb_prompt(qs_b, ks_b, vs_b, batch, seq)
                else:
                    od = _diff_sample(page_table, lam4, sub_g, _token_head_rows(qd_f), _token_head_rows(kd_f),
                                      _token_head_rows(vd_f), ck, cv, nq, lam_init).reshape(-1, hw)
                    osb = _sb_sample(page_table, _head_token_rows(qs_f, n_seq), _head_token_rows(ks_f, n_seq),
                                     _head_token_rows(vs_f, n_seq), sk, sv, nq)
                    osb = osb.reshape(n_seq, SB_HEADS, nq, HEAD_W).transpose(0, 2, 1, 3).reshape(-1, hw)
                mix.append([od, osb])
        else:
            w_in = odd_w_in[li].astype(BF16)
            hw = MOBA_HEADS * HEAD_W
            w_cols = [w_in[:, j * hw:(j + 1) * hw] for j in range(3)]
            w_out = odd_w_out[li].astype(BF16)
            mk, mv = cache_moba_k[li], cache_moba_v[li]
            for gi, gr in enumerate(groups):
                cos, sin = gr["rope128"]
                q_f, q_b = _proj(gr["hb"], w_cols[0], cos, sin, rope=HEAD_W)
                if gi == 0:
                    k_f, k_b, k_mean = _proj(gr["hb"], w_cols[1], cos, sin, rope=HEAD_W, want_mean=True)
                else:
                    k_f, k_b = _proj(gr["hb"], w_cols[1], cos, sin, rope=HEAD_W)
                v_f, v_b = _proj(gr["hb"], w_cols[2], cos, sin)
                new_rows[gi].update(mk=k_f, mv=v_f)
                if gi == 0:
                    k_mean = k_mean.reshape(batch, seq // MOBA_BLOCK, hw)
                    o = _moba_prompt(q_f, q_b, k_mean, k_b, v_b, batch, seq)
                else:
                    o = _moba_sample(page_table, _token_head_rows(q_f), _token_head_rows(k_f),
                                     _token_head_rows(v_f), mk, mv, nq).reshape(-1, hw)
                mix.append([o])
        wqb = peer_w_q[layer].astype(BF16)
        skb = peer_subkeys[layer].astype(BF16)
        ub = peer_u[layer].astype(BF16)
        vtb = peer_v[layer].astype(BF16).T
        for gi, gr in enumerate(groups):
            h, hb = _outproj_ln(mix[gi], w_out, gr["h"], ln_mix_g[layer], ln_mix_b[layer])
            gr["h"], gr["hb"] = _peer_ln(h, hb, wqb, skb, ub, vtb, ln_ffn_g[layer], ln_ffn_b[layer])

    def rows(gi, key, lead, tail):
        return new_rows[gi][key].reshape((1,) + lead + tail)

    lp, ls = (batch, seq), (n_seq, nq)
    t_dk = (DIFF_HEADS, 2, DIFF_DK)
    t_h8 = (DIFF_HEADS, HEAD_W)
    t_h16 = (MOBA_HEADS, HEAD_W)
    return (groups[0]["h"].reshape(batch, seq, d), groups[1]["h"].reshape(n_seq, nq, d),
            rows(0, "dk", lp, t_dk), rows(0, "dv", lp, t_h8), rows(0, "sk", lp, t_h8), rows(0, "sv", lp, t_h8),
            rows(0, "mk", lp, t_h16), rows(0, "mv", lp, t_h16),
            rows(1, "dk", ls, t_dk), rows(1, "dv", ls, t_h8), rows(1, "sk", ls, t_h8), rows(1, "sv", ls, t_h8),
            rows(1, "mk", ls, t_h16), rows(1, "mv", ls, t_h16))
```

```python
import functools
import math

import jax
import jax.numpy as jnp
from jax import lax
from jax.experimental import pallas as pl
from jax.experimental.pallas import tpu as pltpu

F32 = jnp.float32
BF16 = jnp.bfloat16
I32 = jnp.int32

LANES = 128
DIFF_HEADS = 8
DIFF_DK = 64
SB_HEADS = 8
MOBA_HEADS = 16
HEAD_W = 128
MOBA_BLOCK = 256
MOBA_TOPK = 3
ROPE_THETA = 10000.0
PEER_HEADS = 8
PEER_NKEYS = 128
PEER_TOPK = 16
DEPTH = 2
DN_ALPHA = (2 * DEPTH) ** 0.25
LN_EPS = 1e-5
SUBLN_EPS = 1e-5
NEG_INF = float("-inf")

TOKEN_TILE = 256
ATTN_TILE = 256
KEY_TILE = 512
PEER_CHUNK = 1024
SAMPLE_PAGE_GROUP = 4
VMEM_LIMIT = 56 * 1024 * 1024

_NT = (((1,), (1,)), ((), ()))


def _params(sem):
    return pltpu.CompilerParams(dimension_semantics=sem, vmem_limit_bytes=VMEM_LIMIT)


def _dot(a, b):
    return jnp.dot(a, b, preferred_element_type=F32)


def _dot_nt(a, b):
    return lax.dot_general(a, b, _NT, preferred_element_type=F32)


def _split_bf16(x):
    hi = x.astype(BF16)
    lo = (x - hi.astype(F32)).astype(BF16)
    return hi, lo


def _proj_body(x_ref, w_ref, cos_ref, sin_ref, *out_refs, rope, want_mean):
    of_ref, ob_ref = out_refs[0], out_refs[1]
    n = w_ref.shape[1]
    x = x_ref[...]
    for c in range(n // 256):
        acc = _dot(x, w_ref[:, c * 256:(c + 1) * 256])
        if rope:
            parts = []
            for hh in range(2):
                y = acc[:, hh * LANES:(hh + 1) * LANES]
                if rope == LANES:
                    sw = pltpu.roll(y, LANES // 2, 1)
                else:
                    lane = lax.broadcasted_iota(I32, y.shape, 1)
                    first = (lane % rope) < rope // 2
                    sw = jnp.where(first, pltpu.roll(y, LANES - rope // 2, 1),
                                   pltpu.roll(y, rope // 2, 1))
                parts.append(y * cos_ref[...] + sw * sin_ref[...])
            acc = jnp.concatenate(parts, axis=1)
        of_ref[:, c * 256:(c + 1) * 256] = acc
        ob_ref[:, c * 256:(c + 1) * 256] = acc.astype(BF16)
        if want_mean:
            out_refs[2][0, :, c * 256:(c + 1) * 256] = jnp.mean(acc, axis=0, keepdims=True)


def _proj(xb, wb, cos, sin, rope=0, want_mean=False):
    m, k = xb.shape
    n = wb.shape[1]
    tm = TOKEN_TILE
    out_shape = [jax.ShapeDtypeStruct((m, n), F32), jax.ShapeDtypeStruct((m, n), BF16)]
    out_specs = [pl.BlockSpec((tm, n), lambda i: (i, 0)), pl.BlockSpec((tm, n), lambda i: (i, 0))]
    if want_mean:
        out_shape.append(jax.ShapeDtypeStruct((m // tm, 1, n), F32))
        out_specs.append(pl.BlockSpec((1, 1, n), lambda i: (i, 0, 0)))
    return pl.pallas_call(
        functools.partial(_proj_body, rope=rope, want_mean=want_mean),
        grid=(m // tm,),
        in_specs=[pl.BlockSpec((tm, k), lambda i: (i, 0)),
                  pl.BlockSpec((k, n), lambda i: (0, 0)),
                  pl.BlockSpec((tm, LANES), lambda i: (i, 0)),
                  pl.BlockSpec((tm, LANES), lambda i: (i, 0))],
        out_specs=out_specs,
        out_shape=out_shape,
        compiler_params=_params(("parallel",)),
        name="proj",
    )(xb, wb, cos, sin)


def _layernorm_rows(y, g, b):
    mu = jnp.mean(y, axis=-1, keepdims=True)
    d = y - mu
    var = jnp.mean(d * d, axis=-1, keepdims=True)
    return d * lax.rsqrt(var + LN_EPS) * g + b


def _outproj_ln_body(*refs, n_in):
    a_refs = refs[:n_in]
    w_ref, r_ref, g_ref, b_ref, of_ref, ob_ref = refs[n_in:]
    y = DN_ALPHA * r_ref[...]
    off = 0
    for a_ref in a_refs:
        kk = a_ref.shape[1]
        y = y + _dot(a_ref[...].astype(BF16), w_ref[off:off + kk, :])
        off += kk
    o = _layernorm_rows(y, g_ref[...], b_ref[...])
    of_ref[...] = o
    ob_ref[...] = o.astype(BF16)


def _outproj_ln(a_list, wb, resid, g, b):
    m, d = resid.shape
    tm = TOKEN_TILE
    in_specs = [pl.BlockSpec((tm, a.shape[1]), lambda i: (i, 0)) for a in a_list]
    in_specs += [pl.BlockSpec(wb.shape, lambda i: (0, 0)),
                 pl.BlockSpec((tm, d), lambda i: (i, 0)),
                 pl.BlockSpec((1, d), lambda i: (0, 0)),
                 pl.BlockSpec((1, d), lambda i: (0, 0))]
    return pl.pallas_call(
        functools.partial(_outproj_ln_body, n_in=len(a_list)),
        grid=(m // tm,),
        in_specs=in_specs,
        out_specs=[pl.BlockSpec((tm, d), lambda i: (i, 0)), pl.BlockSpec((tm, d), lambda i: (i, 0))],
        out_shape=[jax.ShapeDtypeStruct((m, d), F32), jax.ShapeDtypeStruct((m, d), BF16)],
        compiler_params=_params(("parallel",)),
        name="outproj_ln",
    )(*a_list, wb, resid, g.reshape(1, d), b.reshape(1, d))


def _with_ones(v):
    return jnp.concatenate([v, jnp.ones((v.shape[0], LANES), v.dtype)], axis=1)


def _softmax_update_wide(s, v_ones, m_ref, acc_ref):
    m_prev = m_ref[...]
    m_new = jnp.maximum(m_prev, jnp.max(s, axis=-1, keepdims=True))
    alpha = jnp.exp(m_prev - m_new)
    p = jnp.concatenate([jnp.exp(s[:, c * LANES:(c + 1) * LANES] - m_new)
                         for c in range(s.shape[1] // LANES)], axis=1)
    acc_ref[...] = jnp.concatenate([alpha, alpha], axis=1) * acc_ref[...] + _dot(p.astype(BF16), v_ones)
    m_ref[...] = m_new


def _stack_diff_q(q):
    lane = lax.broadcasted_iota(I32, q.shape, 1)
    zero = jnp.zeros_like(q)
    return jnp.concatenate([jnp.where(lane < DIFF_DK, q, zero),
                            jnp.where(lane >= DIFF_DK, q, zero)], axis=0)


def _diff_finish(acc, rows, lam_ref, g_ref, lam_init):
    o0 = acc[:rows, :LANES] / acc[:rows, LANES:]
    o1 = acc[rows:, :LANES] / acc[rows:, LANES:]
    lam = (jnp.exp(jnp.sum(lam_ref[0:1, :] * lam_ref[1:2, :], axis=-1, keepdims=True))
           - jnp.exp(jnp.sum(lam_ref[2:3, :] * lam_ref[3:4, :], axis=-1, keepdims=True))
           + lam_init)
    o = o0 - lam * o1
    o = o * lax.rsqrt(jnp.mean(o * o, axis=-1, keepdims=True) + SUBLN_EPS)
    return o * g_ref[...] * (1.0 - lam_init)


def _strict_upper_ones(n):
    r = lax.broadcasted_iota(I32, (n, n + LANES), 0)
    c = lax.broadcasted_iota(I32, (n, n + LANES), 1)
    return jnp.where(r > c, 1.0, jnp.where(c >= n, 1.0, 0.0)).astype(BF16)


def _stick_break_weights(z, tri, past, c_ref):
    t = jnp.log(1.0 + jnp.exp(-jnp.abs(z)))
    log_beta = jnp.minimum(z, 0.0) - t
    log_keep = -jnp.maximum(z, 0.0) - t
    if past is not None:
        log_keep = jnp.where(past, log_keep, 0.0)
    hi, lo = _split_bf16(log_keep)
    rows, keys = z.shape
    parts = _dot(jnp.concatenate([hi, lo], axis=0), tri)
    sums = parts[:rows] + parts[rows:]
    carry = c_ref[...]
    a = jnp.concatenate([jnp.exp(log_beta[:, c * LANES:(c + 1) * LANES]
                                 + sums[:, c * LANES:(c + 1) * LANES] + carry)
                         for c in range(keys // LANES)], axis=1)
    if past is not None:
        a = jnp.where(past, a, 0.0)
    c_ref[...] = carry + sums[:, keys:]
    return a.astype(BF16)


def _diff_prompt_body(lam_ref, g_ref, q_ref, k_ref, v_ref, o_ref, acc_ref, m_ref, *, lam_init):
    tq = q_ref.shape[0]
    tk = min(KEY_TILE, k_ref.shape[0])
    i = pl.program_id(2)
    qz = _stack_diff_q(q_ref[...])
    m_ref[...] = jnp.full(m_ref.shape, NEG_INF, F32)
    acc_ref[...] = jnp.zeros(acc_ref.shape, F32)
    scale = DIFF_DK ** -0.5
    n_full = (i * tq) // tk

    def step(j, diagonal):
        rows = pl.ds(pl.multiple_of(j * tk, tk), tk)
        s = _dot_nt(qz, k_ref[rows, :]) * scale
        if diagonal:
            r = lax.broadcasted_iota(I32, s.shape, 0) % tq + i * tq
            c = lax.broadcasted_iota(I32, s.shape, 1) + j * tk
            s = jnp.where(c <= r, s, NEG_INF)
        _softmax_update_wide(s, _with_ones(v_ref[rows, :]), m_ref, acc_ref)

    def body(j, carry):
        step(j, False)
        return carry

    lax.fori_loop(0, n_full, body, 0)
    step(n_full, True)
    o = _diff_finish(acc_ref[...], tq, lam_ref, g_ref, lam_init)
    o_ref[...] = o.astype(BF16)


def _diff_prompt(qb, kb, vb, lam4, g, batch, seq, lam_init):
    tq = ATTN_TILE
    nq = seq // tq
    return pl.pallas_call(
        functools.partial(_diff_prompt_body, lam_init=lam_init),
        grid=(batch, DIFF_HEADS, nq),
        in_specs=[pl.BlockSpec((4, DIFF_DK), lambda b, h, i: (0, 0)),
                  pl.BlockSpec((1, HEAD_W), lambda b, h, i: (0, 0)),
                  pl.BlockSpec((tq, HEAD_W), lambda b, h, i: (b * nq + i, h)),
                  pl.BlockSpec((seq, HEAD_W), lambda b, h, i: (b, h)),
                  pl.BlockSpec((seq, HEAD_W), lambda b, h, i: (b, h))],
        out_specs=pl.BlockSpec((tq, HEAD_W), lambda b, h, i: (b * nq + i, h)),
        out_shape=jax.ShapeDtypeStruct(qb.shape, BF16),
        scratch_shapes=[pltpu.VMEM((2 * tq, 2 * LANES), F32),
                        pltpu.VMEM((2 * tq, LANES), F32)],
        compiler_params=_params(("parallel", "parallel", "arbitrary")),
        name="diff_prompt",
    )(lam4, g, qb, kb, vb)


def _sb_prompt_body(q_ref, k_ref, v_ref, o_ref, acc_ref, c_ref):
    tq = q_ref.shape[0]
    tk = min(KEY_TILE, k_ref.shape[0])
    i = pl.program_id(2)
    q = q_ref[...]
    tri = _strict_upper_ones(tk)
    acc_ref[...] = jnp.zeros(acc_ref.shape, F32)
    c_ref[...] = jnp.zeros(c_ref.shape, F32)
    scale = HEAD_W ** -0.5
    n_full = (i * tq) // tk

    def step(j, diagonal):
        rows = pl.ds(pl.multiple_of(j * tk, tk), tk)
        z = _dot_nt(q, k_ref[rows, :]) * scale
        past = None
        if diagonal:
            r = lax.broadcasted_iota(I32, z.shape, 0) + i * tq
            c = lax.broadcasted_iota(I32, z.shape, 1) + j * tk
            past = c < r
        a = _stick_break_weights(z, tri, past, c_ref)
        acc_ref[...] += _dot(a, v_ref[rows, :])

    step(n_full, True)

    def body(jj, carry):
        step(n_full - 1 - jj, False)
        return carry

    lax.fori_loop(0, n_full, body, 0)
    o_ref[...] = acc_ref[...].astype(BF16)


def _sb_prompt(qb, kb, vb, batch, seq):
    tq = ATTN_TILE
    nq = seq // tq
    return pl.pallas_call(
        _sb_prompt_body,
        grid=(batch, SB_HEADS, nq),
        in_specs=[pl.BlockSpec((tq, HEAD_W), lambda b, h, i: (b * nq + i, h)),
                  pl.BlockSpec((seq, HEAD_W), lambda b, h, i: (b, h)),
                  pl.BlockSpec((seq, HEAD_W), lambda b, h, i: (b, h))],
        out_specs=pl.BlockSpec((tq, HEAD_W), lambda b, h, i: (b * nq + i, h)),
        out_shape=jax.ShapeDtypeStruct(qb.shape, BF16),
        scratch_shapes=[pltpu.VMEM((tq, HEAD_W), F32), pltpu.VMEM((tq, LANES), F32)],
        compiler_params=_params(("parallel", "parallel", "arbitrary")),
        name="sb_prompt",
    )(qb, kb, vb)


def _moba_prompt_body(qf_ref, qb_ref, km_ref, k_ref, v_ref, o_ref, acc_ref, m_ref):
    tq = qb_ref.shape[0]
    tk = min(KEY_TILE, k_ref.shape[0])
    nb = km_ref.shape[0]
    i = pl.program_id(2)
    qb = qb_ref[...]
    scale = HEAD_W ** -0.5

    qh, ql = _split_bf16(qf_ref[...])
    kh, kl = _split_bf16(km_ref[...])
    gate = _dot_nt(qh, kh) + _dot_nt(qh, kl) + _dot_nt(ql, kh)
    n_io = lax.broadcasted_iota(I32, gate.shape, 1)
    gm = jnp.where(n_io < i, gate, NEG_INF)
    rank = jnp.zeros(gate.shape, I32)
    for n2 in range(nb):
        col = gm[:, n2:n2 + 1]
        beats = jnp.where(col > gm, 1, jnp.where(col == gm, jnp.where(n2 < n_io, 1, 0), 0))
        rank = rank + beats
    sel = jnp.where(n_io < i, jnp.where(rank < MOBA_TOPK, 1.0, 0.0), jnp.where(n_io == i, 1.0, 0.0))

    m_ref[...] = jnp.full(m_ref.shape, NEG_INF, F32)
    acc_ref[...] = jnp.zeros(acc_ref.shape, F32)
    blocks_per_tile = tk // tq
    own_tile = i // blocks_per_tile
    r = lax.broadcasted_iota(I32, (tq, tq), 0)
    c = lax.broadcasted_iota(I32, (tq, tq), 1)

    def step(j, has_own):
        rows = pl.ds(pl.multiple_of(j * tk, tk), tk)
        s = _dot_nt(qb, k_ref[rows, :]) * scale
        parts = []
        for bb in range(blocks_per_tile):
            blk = j * blocks_per_tile + bb
            chosen = jnp.sum(jnp.where(n_io == blk, sel, 0.0), axis=-1, keepdims=True)
            sb = jnp.where(chosen > 0.0, s[:, bb * tq:(bb + 1) * tq], NEG_INF)
            if has_own:
                sb = jnp.where(jnp.logical_or(c <= r, blk != i), sb, NEG_INF)
            parts.append(sb)
        _softmax_update_wide(jnp.concatenate(parts, axis=1), _with_ones(v_ref[rows, :]), m_ref, acc_ref)

    step(own_tile, True)

    def body(j, carry):
        step(j, False)
        return carry

    lax.fori_loop(0, own_tile, body, 0)
    acc = acc_ref[...]
    o_ref[...] = (acc[:, :LANES] / acc[:, LANES:]).astype(BF16)


def _moba_prompt(qf, qb, kmean, kb, vb, batch, seq):
    tq = MOBA_BLOCK
    nq = seq // tq
    return pl.pallas_call(
        _moba_prompt_body,
        grid=(batch, MOBA_HEADS, nq),
        in_specs=[pl.BlockSpec((tq, HEAD_W), lambda b, h, i: (b * nq + i, h)),
                  pl.BlockSpec((tq, HEAD_W), lambda b, h, i: (b * nq + i, h)),
                  pl.BlockSpec((None, nq, HEAD_W), lambda b, h, i: (b, 0, h)),
                  pl.BlockSpec((seq, HEAD_W), lambda b, h, i: (b, h)),
                  pl.BlockSpec((seq, HEAD_W), lambda b, h, i: (b, h))],
        out_specs=pl.BlockSpec((tq, HEAD_W), lambda b, h, i: (b * nq + i, h)),
        out_shape=jax.ShapeDtypeStruct(qb.shape, BF16),
        scratch_shapes=[pltpu.VMEM((tq, 2 * LANES), F32),
                        pltpu.VMEM((tq, LANES), F32)],
        compiler_params=_params(("parallel", "parallel", "arbitrary")),
        name="moba_prompt",
    )(qf, qb, kmean, kb, vb)


def _pad_rows(x, n):
    return jnp.concatenate([x, jnp.zeros((n - x.shape[0],) + x.shape[1:], x.dtype)], axis=0)


def _same_head(shape, heads):
    rr = lax.broadcasted_iota(I32, shape, 0)
    cc = lax.broadcasted_iota(I32, shape, 1)
    return (rr & (heads - 1)) == (cc & (heads - 1)), rr, cc


def _page_matrix(ref):
    rows, heads, width = ref.shape
    return ref[...].reshape(rows * heads, width).astype(BF16)


def _diff_sample_body(pt_ref, lam_ref, g_ref, q_ref, kn_ref, vn_ref, *refs, lam_init, group):
    kp_refs, vp_refs = refs[:group], refs[group:2 * group]
    o_ref, acc_ref, m_ref = refs[2 * group:]
    p = pl.program_id(1)
    page, heads, _ = kp_refs[0].shape
    n_rows = q_ref.shape[0]
    shift = heads.bit_length() - 1
    scale = DIFF_DK ** -0.5
    qz = _stack_diff_q(q_ref[...]).astype(BF16)

    @pl.when(p == 0)
    def _():
        m_ref[...] = jnp.full(m_ref.shape, NEG_INF, F32)
        acc_ref[...] = jnp.zeros(acc_ref.shape, F32)

    def update(k2d, v2d, causal):
        s = _dot_nt(qz, k2d) * scale
        ok, rr, cc = _same_head(s.shape, heads)
        s = jnp.where(ok, s, NEG_INF)
        if causal:
            query = (rr & (n_rows - 1)) >> shift
            s = jnp.where((cc >> shift) <= query, s, NEG_INF)
        _softmax_update_wide(s, _with_ones(v2d), m_ref, acc_ref)

    update(jnp.concatenate([_page_matrix(r) for r in kp_refs], axis=0),
           jnp.concatenate([_page_matrix(r) for r in vp_refs], axis=0), False)

    @pl.when(p == pl.num_programs(1) - 1)
    def _():
        update(_pad_rows(kn_ref[...], 2 * n_rows).astype(BF16),
               _pad_rows(vn_ref[...], 2 * n_rows).astype(BF16), True)
        o_ref[...] = _diff_finish(acc_ref[...], n_rows, lam_ref, g_ref, lam_init)


def _page_spec(page, heads, page_of_step):
    return pl.BlockSpec((None, page, heads, HEAD_W), lambda s, p, pt: (pt[s, page_of_step(p)], 0, 0, 0))


def _diff_sample(page_table, lam4, g, q2, k2, v2, cache_k, cache_v, nq, lam_init):
    n_seq, n_pages = page_table.shape
    _, page, heads, _ = cache_k.shape
    n_rows = nq * heads
    group = SAMPLE_PAGE_GROUP
    assert n_rows & (n_rows - 1) == 0 and heads & (heads - 1) == 0 and n_pages % group == 0
    row = pl.BlockSpec((n_rows, HEAD_W), lambda s, p, pt: (s, 0))
    pages = [_page_spec(page, heads, lambda p, j=j: group * p + j) for j in range(group)]
    return pl.pallas_call(
        functools.partial(_diff_sample_body, lam_init=lam_init, group=group),
        grid_spec=pltpu.PrefetchScalarGridSpec(
            num_scalar_prefetch=1,
            grid=(n_seq, n_pages // group),
            in_specs=[pl.BlockSpec((4, DIFF_DK), lambda s, p, pt: (0, 0)),
                      pl.BlockSpec((1, HEAD_W), lambda s, p, pt: (0, 0)),
                      row, row, row] + pages + pages,
            out_specs=row,
            scratch_shapes=[pltpu.VMEM((2 * n_rows, 2 * LANES), F32),
                            pltpu.VMEM((2 * n_rows, LANES), F32)]),
        out_shape=jax.ShapeDtypeStruct(q2.shape, F32),
        compiler_params=_params(("parallel", "arbitrary")),
        name="diff_sample",
    )(page_table, lam4, g, q2, k2, v2, *([cache_k] * group), *([cache_v] * group))


def _sb_sample_body(pt_ref, q_ref, kn_ref, vn_ref, kp_ref, vp_ref, o_ref, acc_ref, c_ref, *, heads):
    p = pl.program_id(1)
    page = kp_ref.shape[0] // heads
    nq = q_ref.shape[0] // heads
    pad = 16
    scale = HEAD_W ** -0.5
    tri = _strict_upper_ones(page)

    def update(get_k, get_v, past):
        zs = []
        for h in range(heads):
            q = _pad_rows(q_ref[h * nq:(h + 1) * nq, :], pad).astype(BF16)
            zs.append(_dot_nt(q, get_k(h).astype(BF16)) * scale)
        a = _stick_break_weights(jnp.concatenate(zs, axis=0), tri, past, c_ref)
        for h in range(heads):
            rows = slice(h * pad, (h + 1) * pad)
            acc_ref[rows, :] += _dot(a[rows], get_v(h).astype(BF16))

    @pl.when(p == 0)
    def _():
        acc_ref[...] = jnp.zeros(acc_ref.shape, F32)
        c_ref[...] = jnp.zeros(c_ref.shape, F32)
        r = lax.broadcasted_iota(I32, (heads * pad, page), 0) & (pad - 1)
        c = lax.broadcasted_iota(I32, (heads * pad, page), 1)
        update(lambda h: _pad_rows(kn_ref[h * nq:(h + 1) * nq, :], page),
               lambda h: _pad_rows(vn_ref[h * nq:(h + 1) * nq, :], page), c < r)

    update(lambda h: kp_ref[pl.ds(h, page, stride=heads), :],
           lambda h: vp_ref[pl.ds(h, page, stride=heads), :], None)

    @pl.when(p == pl.num_programs(1) - 1)
    def _():
        for h in range(heads):
            o_ref[h * nq:(h + 1) * nq, :] = acc_ref[h * pad:h * pad + nq, :]


def _sb_sample(page_table, q2, k2, v2, cache_k, cache_v, nq):
    n_seq, n_pages = page_table.shape
    _, page, heads, _ = cache_k.shape
    row = pl.BlockSpec((nq * heads, HEAD_W), lambda s, p, pt: (s, 0))
    pg = pl.BlockSpec((None, page * heads, HEAD_W), lambda s, p, pt: (pt[s, n_pages - 1 - p], 0, 0))
    rows_view = (cache_k.shape[0], page * heads, HEAD_W)
    return pl.pallas_call(
        functools.partial(_sb_sample_body, heads=heads),
        grid_spec=pltpu.PrefetchScalarGridSpec(
            num_scalar_prefetch=1,
            grid=(n_seq, n_pages),
            in_specs=[row, row, row, pg, pg],
            out_specs=row,
            scratch_shapes=[pltpu.VMEM((heads * 16, HEAD_W), F32),
                            pltpu.VMEM((heads * 16, LANES), F32)]),
        out_shape=jax.ShapeDtypeStruct(q2.shape, F32),
        compiler_params=_params(("parallel", "arbitrary")),
        name="sb_sample",
    )(page_table, q2, k2, v2, cache_k.reshape(rows_view), cache_v.reshape(rows_view))


def _moba_sample_body(pt_ref, q_ref, kn_ref, vn_ref, ka_ref, kb_ref, va_ref, vb_ref, o_ref,
                      o_sc, m_sc, l_sc, g_sc):
    n = pl.program_id(1)
    page, heads, _ = ka_ref.shape
    n_rows = q_ref.shape[0]
    nb = g_sc.shape[0]
    shift = heads.bit_length() - 1
    scale = HEAD_W ** -0.5
    q = q_ref[...]
    qb = q.astype(BF16)

    k3 = jnp.concatenate([ka_ref[...], kb_ref[...]], axis=0)
    v3 = jnp.concatenate([va_ref[...], vb_ref[...]], axis=0)
    k_mean = jnp.mean(k3, axis=0)
    g_sc[n] = jnp.sum(q * jnp.concatenate([k_mean] * (n_rows // heads), axis=0), axis=-1, keepdims=True)
    k2d = k3.reshape(2 * page * heads, HEAD_W).astype(BF16)
    v2d = v3.reshape(2 * page * heads, HEAD_W).astype(BF16)
    s = _dot_nt(qb, k2d) * scale
    ok, _, _ = _same_head(s.shape, heads)
    s = jnp.where(ok, s, NEG_INF)
    m = jnp.max(s, axis=-1, keepdims=True)
    pr = jnp.exp(s - m)
    m_sc[n] = m
    l_sc[n] = jnp.sum(pr, axis=-1, keepdims=True)
    o_sc[n] = _dot(pr.astype(BF16), v2d)

    @pl.when(n == nb - 1)
    def _():
        n_io = lax.broadcasted_iota(I32, (nb, n_rows, 1), 0)
        gm = g_sc[...]
        sel = jnp.zeros(gm.shape, F32)
        for _ in range(MOBA_TOPK):
            mx = jnp.max(gm, axis=0, keepdims=True)
            idx = jnp.min(jnp.where(gm == mx, n_io, nb), axis=0, keepdims=True)
            hit = n_io == idx
            sel = jnp.where(hit, jnp.where(mx > NEG_INF, 1.0, 0.0), sel)
            gm = jnp.where(hit, NEG_INF, gm)
        s_own = _dot_nt(qb, kn_ref[...].astype(BF16)) * scale
        ok_own, rr, cc = _same_head(s_own.shape, heads)
        s_own = jnp.where(ok_own, jnp.where((cc >> shift) <= (rr >> shift), s_own, NEG_INF), NEG_INF)
        m_own = jnp.max(s_own, axis=-1, keepdims=True)
        pr_own = jnp.exp(s_own - m_own)
        l_own = jnp.sum(pr_own, axis=-1, keepdims=True)
        o_own = _dot(pr_own.astype(BF16), vn_ref[...].astype(BF16))
        m_all = m_sc[...]
        m_tot = jnp.maximum(m_own, jnp.max(jnp.where(sel > 0.0, m_all, NEG_INF), axis=0))
        w = jnp.where(sel > 0.0, jnp.exp(m_all - m_tot), 0.0)
        w_own = jnp.exp(m_own - m_tot)
        l_tot = w_own * l_own + jnp.sum(w * l_sc[...], axis=0)
        o_tot = w_own * o_own + jnp.sum(w * o_sc[...], axis=0)
        o_ref[...] = o_tot / l_tot


def _moba_sample(page_table, q2, k2, v2, cache_k, cache_v, nq):
    n_seq, n_pages = page_table.shape
    _, page, heads, _ = cache_k.shape
    n_rows = nq * heads
    assert MOBA_BLOCK == 2 * page and n_pages % 2 == 0 and heads & (heads - 1) == 0
    nb = n_pages // 2
    row = pl.BlockSpec((n_rows, HEAD_W), lambda s, n, pt: (s, 0))
    pg_a = pl.BlockSpec((None, page, heads, HEAD_W), lambda s, n, pt: (pt[s, 2 * n], 0, 0, 0))
    pg_b = pl.BlockSpec((None, page, heads, HEAD_W), lambda s, n, pt: (pt[s, 2 * n + 1], 0, 0, 0))
    return pl.pallas_call(
        _moba_sample_body,
        grid_spec=pltpu.PrefetchScalarGridSpec(
            num_scalar_prefetch=1,
            grid=(n_seq, nb),
            in_specs=[row, row, row, pg_a, pg_b, pg_a, pg_b],
            out_specs=row,
            scratch_shapes=[pltpu.VMEM((nb, n_rows, HEAD_W), F32),
                            pltpu.VMEM((nb, n_rows, 1), F32),
                            pltpu.VMEM((nb, n_rows, 1), F32),
                            pltpu.VMEM((nb, n_rows, 1), F32)]),
        out_shape=jax.ShapeDtypeStruct(q2.shape, F32),
        compiler_params=_params(("parallel", "arbitrary")),
        name="moba_sample",
    )(page_table, q2, k2, v2, cache_k, cache_k, cache_v, cache_v)


def _topk_rows(s, k_top):
    n = s.shape[0]
    row = lax.broadcasted_iota(I32, s.shape, 0)
    rank = jnp.full(s.shape, k_top, I32)
    vals, idxs = [], []
    for k in range(k_top):
        mx = jnp.max(s, axis=0, keepdims=True)
        idx = jnp.min(jnp.where(s == mx, row, n), axis=0, keepdims=True)
        hit = row == idx
        rank = jnp.where(hit, k, rank)
        s = jnp.where(hit, NEG_INF, s)
        vals.append(mx)
        idxs.append(idx)
    return jnp.concatenate(vals, axis=0), jnp.concatenate(idxs, axis=0), rank


def _peer_route_body(x_ref, wq_ref, sk_ref, e0_ref, thr_ref, e1_ref, r1_ref):
    kt = PEER_TOPK
    x = x_ref[...]
    k_io = lax.broadcasted_iota(I32, (kt, x.shape[0]), 0)
    for h in range(PEER_HEADS):
        scores, tops = [], []
        for half in range(2):
            col = (h * 2 + half) * PEER_NKEYS
            q = _dot(x, wq_ref[:, col:col + PEER_NKEYS])
            s = _dot_nt(sk_ref[half], q.astype(BF16))
            scores.append(s)
            tops.append(_topk_rows(s, kt))
        (sv0, _, rank0), (sv1, _, rank1) = tops
        cand = jnp.concatenate([sv0[a:a + 1] + sv1 for a in range(kt)], axis=0)
        _, ci, _ = _topk_rows(cand, kt)
        count = jnp.zeros(k_io.shape, F32)
        for j in range(kt):
            count = count + jnp.where(k_io == (ci[j:j + 1] >> (kt.bit_length() - 1)), 1.0, 0.0)
        w0 = jnp.exp(sv0 - sv0[0:1])
        w1 = jnp.exp(sv1 - sv1[0:1])
        z = jnp.zeros((1, x.shape[0]), F32)
        for a in range(kt):
            inner = jnp.sum(jnp.where(k_io.astype(F32) < count[a:a + 1], w1, 0.0), axis=0, keepdims=True)
            z = z + w0[a:a + 1] * inner
        thr = jnp.zeros(rank0.shape, F32)
        for a in range(kt):
            thr = jnp.where(rank0 == a, count[a:a + 1], thr)
        e0_ref[h] = jnp.where(rank0 < kt, jnp.exp(scores[0] - sv0[0:1]), 0.0) / z
        thr_ref[h] = thr
        e1_ref[h] = jnp.where(rank1 < kt, jnp.exp(scores[1] - sv1[0:1]), 0.0)
        r1_ref[h] = rank1.astype(F32)


def _peer_route(xb, wqb, skb):
    m, d = xb.shape
    tb = TOKEN_TILE
    spec = pl.BlockSpec((PEER_HEADS, PEER_NKEYS, tb), lambda i: (0, 0, i))
    shape = jax.ShapeDtypeStruct((PEER_HEADS, PEER_NKEYS, m), F32)
    return pl.pallas_call(
        _peer_route_body,
        grid=(m // tb,),
        in_specs=[pl.BlockSpec((tb, d), lambda i: (i, 0)),
                  pl.BlockSpec(wqb.shape, lambda i: (0, 0)),
                  pl.BlockSpec(skb.shape, lambda i: (0, 0, 0))],
        out_specs=[spec] * 4,
        out_shape=[shape] * 4,
        compiler_params=_params(("parallel",)),
        name="peer_route",
    )(xb, wqb, skb)


def _gelu(x):
    return 0.5 * x * (1.0 + lax.erf(x * (2.0 ** -0.5)))


def _peer_expert_body(x_ref, e0_ref, thr_ref, e1_ref, r1_ref, u_ref, vt_ref, r_ref, g_ref, b_ref,
                      of_ref, ob_ref, acc_ref):
    c = pl.program_id(1)
    rows_per_chunk = u_ref.shape[0] // PEER_NKEYS

    @pl.when(c == 0)
    def _():
        acc_ref[...] = jnp.zeros(acc_ref.shape, F32)

    x = x_ref[...]
    parts = []
    for rr in range(rows_per_chunk):
        i0 = c * rows_per_chunk + rr
        hid = _dot_nt(u_ref[rr * PEER_NKEYS:(rr + 1) * PEER_NKEYS, :], x)
        w = jnp.zeros(hid.shape, F32)
        for h in range(PEER_HEADS):
            keep = r1_ref[h] < thr_ref[h, pl.ds(i0, 1), :]
            w = w + jnp.where(keep, e1_ref[h], 0.0) * e0_ref[h, pl.ds(i0, 1), :]
        parts.append((_gelu(hid) * w).astype(BF16))
    aw = jnp.concatenate(parts, axis=0)
    acc_ref[...] += _dot(vt_ref[...], aw)

    @pl.when(c == pl.num_programs(1) - 1)
    def _():
        y = DN_ALPHA * r_ref[...] + acc_ref[...].T
        o = _layernorm_rows(y, g_ref[...], b_ref[...])
        of_ref[...] = o
        ob_ref[...] = o.astype(BF16)


def _peer_expert(xb, routes, ub, vtb, resid, g, b):
    m, d = xb.shape
    n_exp = ub.shape[0]
    tb = TOKEN_TILE
    ec = PEER_CHUNK
    rspec = pl.BlockSpec((PEER_HEADS, PEER_NKEYS, tb), lambda i, c: (0, 0, i))
    return pl.pallas_call(
        _peer_expert_body,
        grid=(m // tb, n_exp // ec),
        in_specs=[pl.BlockSpec((tb, d), lambda i, c: (i, 0)),
                  rspec, rspec, rspec, rspec,
                  pl.BlockSpec((ec, d), lambda i, c: (c, 0)),
                  pl.BlockSpec((d, ec), lambda i, c: (0, c)),
                  pl.BlockSpec((tb, d), lambda i, c: (i, 0)),
                  pl.BlockSpec((1, d), lambda i, c: (0, 0)),
                  pl.BlockSpec((1, d), lambda i, c: (0, 0))],
        out_specs=[pl.BlockSpec((tb, d), lambda i, c: (i, 0)), pl.BlockSpec((tb, d), lambda i, c: (i, 0))],
        out_shape=[jax.ShapeDtypeStruct((m, d), F32), jax.ShapeDtypeStruct((m, d), BF16)],
        scratch_shapes=[pltpu.VMEM((d, tb), F32)],
        compiler_params=_params(("parallel", "arbitrary")),
        name="peer_expert",
    )(xb, *routes, ub, vtb, resid, g.reshape(1, d), b.reshape(1, d))


def _peer_ln(h, hb, wqb, skb, ub, vtb, g, b):
    routes = _peer_route(hb, wqb, skb)
    return _peer_expert(hb, routes, ub, vtb, h, g, b)


def _token_head_rows(x):
    return x.reshape(-1, HEAD_W)


def _head_token_rows(x, n_seq):
    tokens = x.shape[0] // n_seq
    heads = x.shape[1] // HEAD_W
    return x.reshape(n_seq, tokens, heads, HEAD_W).transpose(0, 2, 1, 3).reshape(-1, HEAD_W)


def _rope_tables(pos, width):
    half = width // 2
    inv = ROPE_THETA ** (-jnp.arange(half, dtype=F32) / half)
    ang = pos.astype(F32)[:, None] * inv[None, :]
    cos = jnp.tile(jnp.cos(ang), (1, LANES // half))
    sin = jnp.sin(ang)
    sin = jnp.tile(jnp.concatenate([-sin, sin], axis=1), (1, LANES // width))
    return cos, sin


def kernel(x_prompt, x_sample, cache_diff_k, cache_diff_v, cache_sb_k, cache_sb_v, cache_moba_k,
           cache_moba_v, page_table, even_w_in, even_w_out, diff_lambda_q1, diff_lambda_k1,
           diff_lambda_q2, diff_lambda_k2, diff_subln_g, odd_w_in, odd_w_out, ln_mix_g, ln_mix_b,
           ln_ffn_g, ln_ffn_b, peer_w_q, peer_subkeys, peer_u, peer_v):
    batch, seq, d = x_prompt.shape
    n_seq, nq, _ = x_sample.shape
    page = cache_diff_k.shape[2]
    past_len = page_table.shape[1] * page
    depth = ln_mix_g.shape[0]
    assert depth == DEPTH and seq % ATTN_TILE == 0 and past_len % MOBA_BLOCK == 0
    assert (batch * seq) % TOKEN_TILE == 0 and (n_seq * nq) % TOKEN_TILE == 0 and nq == 8

    pos_p = jnp.tile(jnp.arange(seq, dtype=I32), batch)
    pos_s = jnp.tile(past_len + jnp.arange(nq, dtype=I32), n_seq)
    groups = [
        dict(h=x_prompt.reshape(batch * seq, d), pos=pos_p),
        dict(h=x_sample.reshape(n_seq * nq, d), pos=pos_s),
    ]
    for gr in groups:
        gr["hb"] = gr["h"].astype(BF16)
        gr["rope64"] = _rope_tables(gr["pos"], DIFF_DK)
        gr["rope128"] = _rope_tables(gr["pos"], HEAD_W)
    new_rows = [dict(), dict()]

    for layer in range(depth):
        li = layer // 2
        mix = []
        if layer % 2 == 0:
            lam_init = 0.8 - 0.6 * math.exp(-0.3 * layer)
            lam4 = jnp.stack([diff_lambda_q1[li], diff_lambda_k1[li], diff_lambda_q2[li], diff_lambda_k2[li]])
            sub_g = diff_subln_g[li].reshape(1, HEAD_W)
            w_in = even_w_in[li].astype(BF16)
            hw = DIFF_HEADS * HEAD_W
            w_cols = [w_in[:, j * hw:(j + 1) * hw] for j in range(6)]
            w_out = even_w_out[li].astype(BF16)
            ck = cache_diff_k[li].reshape(-1, page, DIFF_HEADS, HEAD_W)
            cv, sk, sv = cache_diff_v[li], cache_sb_k[li], cache_sb_v[li]
            for gi, gr in enumerate(groups):
                cos, sin = gr["rope64"]
                qd_f, qd_b = _proj(gr["hb"], w_cols[0], cos, sin, rope=DIFF_DK)
                kd_f, kd_b = _proj(gr["hb"], w_cols[1], cos, sin, rope=DIFF_DK)
                vd_f, vd_b = _proj(gr["hb"], w_cols[2], cos, sin)
                qs_f, qs_b = _proj(gr["hb"], w_cols[3], cos, sin)
                ks_f, ks_b = _proj(gr["hb"], w_cols[4], cos, sin)
                vs_f, vs_b = _proj(gr["hb"], w_cols[5], cos, sin)
                new_rows[gi].update(dk=kd_f, dv=vd_f, sk=ks_f, sv=vs_f)
                if gi == 0:
                    od = _diff_prompt(qd_b, kd_b, vd_b, lam4, sub_g, batch, seq, lam_init)
                    osb = _sb_prompt(qs_b, ks_b, vs_b, batch, seq)
                else:
                    od = _diff_sample(page_table, lam4, sub_g, _token_head_rows(qd_f), _token_head_rows(kd_f),
                                      _token_head_rows(vd_f), ck, cv, nq, lam_init).reshape(-1, hw)
                    osb = _sb_sample(page_table, _head_token_rows(qs_f, n_seq), _head_token_rows(ks_f, n_seq),
                                     _head_token_rows(vs_f, n_seq), sk, sv, nq)
                    osb = osb.reshape(n_seq, SB_HEADS, nq, HEAD_W).transpose(0, 2, 1, 3).reshape(-1, hw)
                mix.append([od, osb])
        else:
            w_in = odd_w_in[li].astype(BF16)
            hw = MOBA_HEADS * HEAD_W
            w_cols = [w_in[:, j * hw:(j + 1) * hw] for j in range(3)]
            w_out = odd_w_out[li].astype(BF16)
            mk, mv = cache_moba_k[li], cache_moba_v[li]
            for gi, gr in enumerate(groups):
                cos, sin = gr["rope128"]
                q_f, q_b = _proj(gr["hb"], w_cols[0], cos, sin, rope=HEAD_W)
                if gi == 0:
                    k_f, k_b, k_mean = _proj(gr["hb"], w_cols[1], cos, sin, rope=HEAD_W, want_mean=True)
                else:
                    k_f, k_b = _proj(gr["hb"], w_cols[1], cos, sin, rope=HEAD_W)
                v_f, v_b = _proj(gr["hb"], w_cols[2], cos, sin)
                new_rows[gi].update(mk=k_f, mv=v_f)
                if gi == 0:
                    k_mean = k_mean.reshape(batch, seq // MOBA_BLOCK, hw)
                    o = _moba_prompt(q_f, q_b, k_mean, k_b, v_b, batch, seq)
                else:
                    o = _moba_sample(page_table, _token_head_rows(q_f), _token_head_rows(k_f),
                                     _token_head_rows(v_f), mk, mv, nq).reshape(-1, hw)
                mix.append([o])
        wqb = peer_w_q[layer].astype(BF16)
        skb = peer_subkeys[layer].astype(BF16)
        ub = peer_u[layer].astype(BF16)
        vtb = peer_v[layer].astype(BF16).T
        for gi, gr in enumerate(groups):
            h, hb = _outproj_ln(mix[gi], w_out, gr["h"], ln_mix_g[layer], ln_mix_b[layer])
            gr["h"], gr["hb"] = _peer_ln(h, hb, wqb, skb, ub, vtb, ln_ffn_g[layer], ln_ffn_b[layer])

    def rows(gi, key, lead, tail):
        return new_rows[gi][key].reshape((1,) + lead + tail)

    lp, ls = (batch, seq), (n_seq, nq)
    t_dk = (DIFF_HEADS, 2, DIFF_DK)
    t_h8 = (DIFF_HEADS, HEAD_W)
    t_h16 = (MOBA_HEADS, HEAD_W)
    return (groups[0]["h"].reshape(batch, seq, d), groups[1]["h"].reshape(n_seq, nq, d),
            rows(0, "dk", lp, t_dk), rows(0, "dv", lp, t_h8), rows(0, "sk", lp, t_h8), rows(0, "sv", lp, t_h8),
            rows(0, "mk", lp, t_h16), rows(0, "mv", lp, t_h16),
            rows(1, "dk", ls, t_dk), rows(1, "dv", ls, t_h8), rows(1, "sk", ls, t_h8), rows(1, "sv", ls, t_h8),
            rows(1, "mk", ls, t_h16), rows(1, "mv", ls, t_h16))
```

```python
import functools
import math

import jax
import jax.numpy as jnp
from jax import lax
from jax.experimental import pallas as pl
from jax.experimental.pallas import tpu as pltpu

F32 = jnp.float32
BF16 = jnp.bfloat16
I32 = jnp.int32

LANES = 128
DIFF_HEADS = 8
DIFF_DK = 64
SB_HEADS = 8
MOBA_HEADS = 16
HEAD_W = 128
MOBA_BLOCK = 256
MOBA_TOPK = 3
ROPE_THETA = 10000.0
PEER_HEADS = 8
PEER_NKEYS = 128
PEER_TOPK = 16
DEPTH = 2
DN_ALPHA = (2 * DEPTH) ** 0.25
LN_EPS = 1e-5
SUBLN_EPS = 1e-5
NEG_INF = float("-inf")

TOKEN_TILE = 256
ATTN_TILE = 256
KEY_TILE = 512
PEER_CHUNK = 1024
SAMPLE_PAGE_GROUP = 4
VMEM_LIMIT = 56 * 1024 * 1024

_NT = (((1,), (1,)), ((), ()))


def _params(sem):
    return pltpu.CompilerParams(dimension_semantics=sem, vmem_limit_bytes=VMEM_LIMIT)


def _dot(a, b):
    return jnp.dot(a, b, preferred_element_type=F32)


def _dot_nt(a, b):
    return lax.dot_general(a, b, _NT, preferred_element_type=F32)


def _split_bf16(x):
    hi = x.astype(BF16)
    lo = (x - hi.astype(F32)).astype(BF16)
    return hi, lo


def _proj_body(x_ref, w_ref, cos_ref, sin_ref, *out_refs, rope, want_mean):
    of_ref, ob_ref = out_refs[0], out_refs[1]
    n = w_ref.shape[1]
    x = x_ref[...]
    for c in range(n // 256):
        acc = _dot(x, w_ref[:, c * 256:(c + 1) * 256])
        if rope:
            parts = []
            for hh in range(2):
                y = acc[:, hh * LANES:(hh + 1) * LANES]
                if rope == LANES:
                    sw = pltpu.roll(y, LANES // 2, 1)
                else:
                    lane = lax.broadcasted_iota(I32, y.shape, 1)
                    first = (lane % rope) < rope // 2
                    sw = jnp.where(first, pltpu.roll(y, LANES - rope // 2, 1),
                                   pltpu.roll(y, rope // 2, 1))
                parts.append(y * cos_ref[...] + sw * sin_ref[...])
            acc = jnp.concatenate(parts, axis=1)
        of_ref[:, c * 256:(c + 1) * 256] = acc
        ob_ref[:, c * 256:(c + 1) * 256] = acc.astype(BF16)
        if want_mean:
            out_refs[2][0, :, c * 256:(c + 1) * 256] = jnp.mean(acc, axis=0, keepdims=True)


def _proj(xb, wb, cos, sin, rope=0, want_mean=False):
    m, k = xb.shape
    n = wb.shape[1]
    tm = TOKEN_TILE
    out_shape = [jax.ShapeDtypeStruct((m, n), F32), jax.ShapeDtypeStruct((m, n), BF16)]
    out_specs = [pl.BlockSpec((tm, n), lambda i: (i, 0)), pl.BlockSpec((tm, n), lambda i: (i, 0))]
    if want_mean:
        out_shape.append(jax.ShapeDtypeStruct((m // tm, 1, n), F32))
        out_specs.append(pl.BlockSpec((1, 1, n), lambda i: (i, 0, 0)))
    return pl.pallas_call(
        functools.partial(_proj_body, rope=rope, want_mean=want_mean),
        grid=(m // tm,),
        in_specs=[pl.BlockSpec((tm, k), lambda i: (i, 0)),
                  pl.BlockSpec((k, n), lambda i: (0, 0)),
                  pl.BlockSpec((tm, LANES), lambda i: (i, 0)),
                  pl.BlockSpec((tm, LANES), lambda i: (i, 0))],
        out_specs=out_specs,
        out_shape=out_shape,
        compiler_params=_params(("parallel",)),
        name="proj",
    )(xb, wb, cos, sin)


def _layernorm_rows(y, g, b):
    mu = jnp.mean(y, axis=-1, keepdims=True)
    d = y - mu
    var = jnp.mean(d * d, axis=-1, keepdims=True)
    return d * lax.rsqrt(var + LN_EPS) * g + b


def _outproj_ln_body(*refs, n_in):
    a_refs = refs[:n_in]
    w_ref, r_ref, g_ref, b_ref, of_ref, ob_ref = refs[n_in:]
    y = DN_ALPHA * r_ref[...]
    off = 0
    for a_ref in a_refs:
        kk = a_ref.shape[1]
        y = y + _dot(a_ref[...].astype(BF16), w_ref[off:off + kk, :])
        off += kk
    o = _layernorm_rows(y, g_ref[...], b_ref[...])
    of_ref[...] = o
    ob_ref[...] = o.astype(BF16)


def _outproj_ln(a_list, wb, resid, g, b):
    m, d = resid.shape
    tm = TOKEN_TILE
    in_specs = [pl.BlockSpec((tm, a.shape[1]), lambda i: (i, 0)) for a in a_list]
    in_specs += [pl.BlockSpec(wb.shape, lambda i: (0, 0)),
                 pl.BlockSpec((tm, d), lambda i: (i, 0)),
                 pl.BlockSpec((1, d), lambda i: (0, 0)),
                 pl.BlockSpec((1, d), lambda i: (0, 0))]
    return pl.pallas_call(
        functools.partial(_outproj_ln_body, n_in=len(a_list)),
        grid=(m // tm,),
        in_specs=in_specs,
        out_specs=[pl.BlockSpec((tm, d), lambda i: (i, 0)), pl.BlockSpec((tm, d), lambda i: (i, 0))],
        out_shape=[jax.ShapeDtypeStruct((m, d), F32), jax.ShapeDtypeStruct((m, d), BF16)],
        compiler_params=_params(("parallel",)),
        name="outproj_ln",
    )(*a_list, wb, resid, g.reshape(1, d), b.reshape(1, d))


def _with_ones(v):
    return jnp.concatenate([v, jnp.ones((v.shape[0], LANES), v.dtype)], axis=1)


def _softmax_update_wide(s, v_ones, m_ref, acc_ref):
    m_prev = m_ref[...]
    m_new = jnp.maximum(m_prev, jnp.max(s, axis=-1, keepdims=True))
    alpha = jnp.exp(m_prev - m_new)
    p = jnp.concatenate([jnp.exp(s[:, c * LANES:(c + 1) * LANES] - m_new)
                         for c in range(s.shape[1] // LANES)], axis=1)
    acc_ref[...] = jnp.concatenate([alpha, alpha], axis=1) * acc_ref[...] + _dot(p.astype(BF16), v_ones)
    m_ref[...] = m_new


def _stack_diff_q(q):
    lane = lax.broadcasted_iota(I32, q.shape, 1)
    zero = jnp.zeros_like(q)
    return jnp.concatenate([jnp.where(lane < DIFF_DK, q, zero),
                            jnp.where(lane >= DIFF_DK, q, zero)], axis=0)


def _diff_finish(acc, rows, lam_ref, g_ref, lam_init):
    o0 = acc[:rows, :LANES] / acc[:rows, LANES:]
    o1 = acc[rows:, :LANES] / acc[rows:, LANES:]
    lam = (jnp.exp(jnp.sum(lam_ref[0:1, :] * lam_ref[1:2, :], axis=-1, keepdims=True))
           - jnp.exp(jnp.sum(lam_ref[2:3, :] * lam_ref[3:4, :], axis=-1, keepdims=True))
           + lam_init)
    o = o0 - lam * o1
    o = o * lax.rsqrt(jnp.mean(o * o, axis=-1, keepdims=True) + SUBLN_EPS)
    return o * g_ref[...] * (1.0 - lam_init)


def _strict_upper_ones(n):
    r = lax.broadcasted_iota(I32, (n, n + LANES), 0)
    c = lax.broadcasted_iota(I32, (n, n + LANES), 1)
    return jnp.where(r > c, 1.0, jnp.where(c >= n, 1.0, 0.0)).astype(BF16)


def _stick_break_weights(z, tri, past, c_ref):
    t = jnp.log(1.0 + jnp.exp(-jnp.abs(z)))
    log_beta = jnp.minimum(z, 0.0) - t
    log_keep = -jnp.maximum(z, 0.0) - t
    if past is not None:
        log_keep = jnp.where(past, log_keep, 0.0)
    hi, lo = _split_bf16(log_keep)
    rows, keys = z.shape
    parts = _dot(jnp.concatenate([hi, lo], axis=0), tri)
    sums = parts[:rows] + parts[rows:]
    carry = c_ref[...]
    a = jnp.concatenate([jnp.exp(log_beta[:, c * LANES:(c + 1) * LANES]
                                 + sums[:, c * LANES:(c + 1) * LANES] + carry)
                         for c in range(keys // LANES)], axis=1)
    if past is not None:
        a = jnp.where(past, a, 0.0)
    c_ref[...] = carry + sums[:, keys:]
    return a.astype(BF16)


def _diff_prompt_body(lam_ref, g_ref, q_ref, k_ref, v_ref, o_ref, acc_ref, m_ref, *, lam_init):
    tq = q_ref.shape[0]
    tk = min(KEY_TILE, k_ref.shape[0])
    i = pl.program_id(2)
    qz = _stack_diff_q(q_ref[...])
    m_ref[...] = jnp.full(m_ref.shape, NEG_INF, F32)
    acc_ref[...] = jnp.zeros(acc_ref.shape, F32)
    scale = DIFF_DK ** -0.5
    n_full = (i * tq) // tk

    def step(j, diagonal):
        rows = pl.ds(pl.multiple_of(j * tk, tk), tk)
        s = _dot_nt(qz, k_ref[rows, :]) * scale
        if diagonal:
            r = lax.broadcasted_iota(I32, s.shape, 0) % tq + i * tq
            c = lax.broadcasted_iota(I32, s.shape, 1) + j * tk
            s = jnp.where(c <= r, s, NEG_INF)
        _softmax_update_wide(s, _with_ones(v_ref[rows, :]), m_ref, acc_ref)

    def body(j, carry):
        step(j, False)
        return carry

    lax.fori_loop(0, n_full, body, 0)
    step(n_full, True)
    o = _diff_finish(acc_ref[...], tq, lam_ref, g_ref, lam_init)
    o_ref[...] = o.astype(BF16)


def _diff_prompt(qb, kb, vb, lam4, g, batch, seq, lam_init):
    tq = ATTN_TILE
    nq = seq // tq
    return pl.pallas_call(
        functools.partial(_diff_prompt_body, lam_init=lam_init),
        grid=(batch, DIFF_HEADS, nq),
        in_specs=[pl.BlockSpec((4, DIFF_DK), lambda b, h, i: (0, 0)),
                  pl.BlockSpec((1, HEAD_W), lambda b, h, i: (0, 0)),
                  pl.BlockSpec((tq, HEAD_W), lambda b, h, i: (b * nq + i, h)),
                  pl.BlockSpec((seq, HEAD_W), lambda b, h, i: (b, h)),
                  pl.BlockSpec((seq, HEAD_W), lambda b, h, i: (b, h))],
        out_specs=pl.BlockSpec((tq, HEAD_W), lambda b, h, i: (b * nq + i, h)),
        out_shape=jax.ShapeDtypeStruct(qb.shape, BF16),
        scratch_shapes=[pltpu.VMEM((2 * tq, 2 * LANES), F32),
                        pltpu.VMEM((2 * tq, LANES), F32)],
        compiler_params=_params(("parallel", "parallel", "arbitrary")),
        name="diff_prompt",
    )(lam4, g, qb, kb, vb)


def _sb_prompt_body(q_ref, k_ref, v_ref, o_ref, acc_ref, c_ref):
    tq = q_ref.shape[0]
    tk = min(KEY_TILE, k_ref.shape[0])
    i = pl.program_id(2)
    q = q_ref[...]
    tri = _strict_upper_ones(tk)
    acc_ref[...] = jnp.zeros(acc_ref.shape, F32)
    c_ref[...] = jnp.zeros(c_ref.shape, F32)
    scale = HEAD_W ** -0.5
    n_full = (i * tq) // tk

    def step(j, diagonal):
        rows = pl.ds(pl.multiple_of(j * tk, tk), tk)
        z = _dot_nt(q, k_ref[rows, :]) * scale
        past = None
        if diagonal:
            r = lax.broadcasted_iota(I32, z.shape, 0) + i * tq
            c = lax.broadcasted_iota(I32, z.shape, 1) + j * tk
            past = c < r
        a = _stick_break_weights(z, tri, past, c_ref)
        acc_ref[...] += _dot(a, v_ref[rows, :])

    step(n_full, True)

    def body(jj, carry):
        step(n_full - 1 - jj, False)
        return carry

    lax.fori_loop(0, n_full, body, 0)
    o_ref[...] = acc_ref[...].astype(BF16)


def _sb_prompt(qb, kb, vb, batch, seq):
    tq = ATTN_TILE
    nq = seq // tq
    return pl.pallas_call(
        _sb_prompt_body,
        grid=(batch, SB_HEADS, nq),
        in_specs=[pl.BlockSpec((tq, HEAD_W), lambda b, h, i: (b * nq + i, h)),
                  pl.BlockSpec((seq, HEAD_W), lambda b, h, i: (b, h)),
                  pl.BlockSpec((seq, HEAD_W), lambda b, h, i: (b, h))],
        out_specs=pl.BlockSpec((tq, HEAD_W), lambda b, h, i: (b * nq + i, h)),
        out_shape=jax.ShapeDtypeStruct(qb.shape, BF16),
        scratch_shapes=[pltpu.VMEM((tq, HEAD_W), F32), pltpu.VMEM((tq, LANES), F32)],
        compiler_params=_params(("parallel", "parallel", "arbitrary")),
        name="sb_prompt",
    )(qb, kb, vb)


def _moba_prompt_body(qf_ref, qb_ref, km_ref, k_ref, v_ref, o_ref, acc_ref, m_ref):
    tq = qb_ref.shape[0]
    tk = min(KEY_TILE, k_ref.shape[0])
    nb = km_ref.shape[0]
    i = pl.program_id(2)
    qb = qb_ref[...]
    scale = HEAD_W ** -0.5

    qh, ql = _split_bf16(qf_ref[...])
    kh, kl = _split_bf16(km_ref[...])
    gate = _dot_nt(kh, qh) + _dot_nt(kl, qh) + _dot_nt(kh, ql)
    n_io = lax.broadcasted_iota(I32, gate.shape, 0)
    gm = jnp.where(n_io < i, gate, NEG_INF)
    rank = jnp.zeros(gate.shape, I32)
    for n2 in range(nb):
        row = gm[n2:n2 + 1, :]
        beats = jnp.where(row > gm, 1, jnp.where(row == gm, jnp.where(n2 < n_io, 1, 0), 0))
        rank = rank + beats
    sel_t = jnp.where(n_io < i, jnp.where(rank < MOBA_TOPK, 1.0, 0.0), jnp.where(n_io == i, 1.0, 0.0))
    sel = _pad_rows(sel_t, LANES).T.astype(BF16)

    m_ref[...] = jnp.full(m_ref.shape, NEG_INF, F32)
    acc_ref[...] = jnp.zeros(acc_ref.shape, F32)
    blocks_per_tile = tk // tq
    own_tile = i // blocks_per_tile
    shift = tq.bit_length() - 1
    r = lax.broadcasted_iota(I32, (tq, tk), 0)
    c = lax.broadcasted_iota(I32, (tq, tk), 1)
    blk_row = lax.broadcasted_iota(I32, (LANES, tk), 0)
    blk_col = lax.broadcasted_iota(I32, (LANES, tk), 1) >> shift

    def step(j, has_own):
        rows = pl.ds(pl.multiple_of(j * tk, tk), tk)
        s = _dot_nt(qb, k_ref[rows, :]) * scale
        expand = jnp.where(blk_row == blk_col + j * blocks_per_tile, 1.0, 0.0).astype(BF16)
        s = jnp.where(_dot(sel, expand) > 0.5, s, NEG_INF)
        if has_own:
            own_block = (c >> shift) + j * blocks_per_tile == i
            s = jnp.where(jnp.logical_or((c & (tq - 1)) <= r, jnp.logical_not(own_block)), s, NEG_INF)
        _softmax_update_wide(s, _with_ones(v_ref[rows, :]), m_ref, acc_ref)

    step(own_tile, True)

    def body(j, carry):
        step(j, False)
        return carry

    lax.fori_loop(0, own_tile, body, 0)
    acc = acc_ref[...]
    o_ref[...] = (acc[:, :LANES] / acc[:, LANES:]).astype(BF16)


def _moba_prompt(qf, qb, kmean, kb, vb, batch, seq):
    tq = MOBA_BLOCK
    nq = seq // tq
    return pl.pallas_call(
        _moba_prompt_body,
        grid=(batch, MOBA_HEADS, nq),
        in_specs=[pl.BlockSpec((tq, HEAD_W), lambda b, h, i: (b * nq + i, h)),
                  pl.BlockSpec((tq, HEAD_W), lambda b, h, i: (b * nq + i, h)),
                  pl.BlockSpec((None, nq, HEAD_W), lambda b, h, i: (b, 0, h)),
                  pl.BlockSpec((seq, HEAD_W), lambda b, h, i: (b, h)),
                  pl.BlockSpec((seq, HEAD_W), lambda b, h, i: (b, h))],
        out_specs=pl.BlockSpec((tq, HEAD_W), lambda b, h, i: (b * nq + i, h)),
        out_shape=jax.ShapeDtypeStruct(qb.shape, BF16),
        scratch_shapes=[pltpu.VMEM((tq, 2 * LANES), F32),
                        pltpu.VMEM((tq, LANES), F32)],
        compiler_params=_params(("parallel", "parallel", "arbitrary")),
        name="moba_prompt",
    )(qf, qb, kmean, kb, vb)


def _pad_rows(x, n):
    return jnp.concatenate([x, jnp.zeros((n - x.shape[0],) + x.shape[1:], x.dtype)], axis=0)


def _same_head(shape, heads):
    rr = lax.broadcasted_iota(I32, shape, 0)
    cc = lax.broadcasted_iota(I32, shape, 1)
    return (rr & (heads - 1)) == (cc & (heads - 1)), rr, cc


def _page_matrix(ref):
    rows, heads, width = ref.shape
    return ref[...].reshape(rows * heads, width).astype(BF16)


def _diff_sample_body(pt_ref, lam_ref, g_ref, q_ref, kn_ref, vn_ref, *refs, lam_init, group):
    kp_refs, vp_refs = refs[:group], refs[group:2 * group]
    o_ref, acc_ref, m_ref = refs[2 * group:]
    p = pl.program_id(1)
    page, heads, _ = kp_refs[0].shape
    n_rows = q_ref.shape[0]
    shift = heads.bit_length() - 1
    scale = DIFF_DK ** -0.5
    qz = _stack_diff_q(q_ref[...]).astype(BF16)

    @pl.when(p == 0)
    def _():
        m_ref[...] = jnp.full(m_ref.shape, NEG_INF, F32)
        acc_ref[...] = jnp.zeros(acc_ref.shape, F32)

    def update(k2d, v2d, causal):
        s = _dot_nt(qz, k2d) * scale
        ok, rr, cc = _same_head(s.shape, heads)
        s = jnp.where(ok, s, NEG_INF)
        if causal:
            query = (rr & (n_rows - 1)) >> shift
            s = jnp.where((cc >> shift) <= query, s, NEG_INF)
        _softmax_update_wide(s, _with_ones(v2d), m_ref, acc_ref)

    update(jnp.concatenate([_page_matrix(r) for r in kp_refs], axis=0),
           jnp.concatenate([_page_matrix(r) for r in vp_refs], axis=0), False)

    @pl.when(p == pl.num_programs(1) - 1)
    def _():
        update(_pad_rows(kn_ref[...], 2 * n_rows).astype(BF16),
               _pad_rows(vn_ref[...], 2 * n_rows).astype(BF16), True)
        o_ref[...] = _diff_finish(acc_ref[...], n_rows, lam_ref, g_ref, lam_init)


def _page_spec(page, heads, page_of_step):
    return pl.BlockSpec((None, page, heads, HEAD_W), lambda s, p, pt: (pt[s, page_of_step(p)], 0, 0, 0))


def _diff_sample(page_table, lam4, g, q2, k2, v2, cache_k, cache_v, nq, lam_init):
    n_seq, n_pages = page_table.shape
    _, page, heads, _ = cache_k.shape
    n_rows = nq * heads
    group = SAMPLE_PAGE_GROUP
    assert n_rows & (n_rows - 1) == 0 and heads & (heads - 1) == 0 and n_pages % group == 0
    row = pl.BlockSpec((n_rows, HEAD_W), lambda s, p, pt: (s, 0))
    pages = [_page_spec(page, heads, lambda p, j=j: group * p + j) for j in range(group)]
    return pl.pallas_call(
        functools.partial(_diff_sample_body, lam_init=lam_init, group=group),
        grid_spec=pltpu.PrefetchScalarGridSpec(
            num_scalar_prefetch=1,
            grid=(n_seq, n_pages // group),
            in_specs=[pl.BlockSpec((4, DIFF_DK), lambda s, p, pt: (0, 0)),
                      pl.BlockSpec((1, HEAD_W), lambda s, p, pt: (0, 0)),
                      row, row, row] + pages + pages,
            out_specs=row,
            scratch_shapes=[pltpu.VMEM((2 * n_rows, 2 * LANES), F32),
                            pltpu.VMEM((2 * n_rows, LANES), F32)]),
        out_shape=jax.ShapeDtypeStruct(q2.shape, F32),
        compiler_params=_params(("parallel", "arbitrary")),
        name="diff_sample",
    )(page_table, lam4, g, q2, k2, v2, *([cache_k] * group), *([cache_v] * group))


def _sb_sample_body(pt_ref, q_ref, kn_ref, vn_ref, kp_ref, vp_ref, o_ref, acc_ref, c_ref, *, heads):
    p = pl.program_id(1)
    page = kp_ref.shape[0] // heads
    nq = q_ref.shape[0] // heads
    pad = 16
    scale = HEAD_W ** -0.5
    tri = _strict_upper_ones(page)

    def update(get_k, get_v, past):
        zs = []
        for h in range(heads):
            q = _pad_rows(q_ref[h * nq:(h + 1) * nq, :], pad).astype(BF16)
            zs.append(_dot_nt(q, get_k(h).astype(BF16)) * scale)
        a = _stick_break_weights(jnp.concatenate(zs, axis=0), tri, past, c_ref)
        for h in range(heads):
            rows = slice(h * pad, (h + 1) * pad)
            acc_ref[rows, :] += _dot(a[rows], get_v(h).astype(BF16))

    @pl.when(p == 0)
    def _():
        acc_ref[...] = jnp.zeros(acc_ref.shape, F32)
        c_ref[...] = jnp.zeros(c_ref.shape, F32)
        r = lax.broadcasted_iota(I32, (heads * pad, page), 0) & (pad - 1)
        c = lax.broadcasted_iota(I32, (heads * pad, page), 1)
        update(lambda h: _pad_rows(kn_ref[h * nq:(h + 1) * nq, :], page),
               lambda h: _pad_rows(vn_ref[h * nq:(h + 1) * nq, :], page), c < r)

    update(lambda h: kp_ref[pl.ds(h, page, stride=heads), :],
           lambda h: vp_ref[pl.ds(h, page, stride=heads), :], None)

    @pl.when(p == pl.num_programs(1) - 1)
    def _():
        for h in range(heads):
            o_ref[h * nq:(h + 1) * nq, :] = acc_ref[h * pad:h * pad + nq, :]


def _sb_sample(page_table, q2, k2, v2, cache_k, cache_v, nq):
    n_seq, n_pages = page_table.shape
    _, page, heads, _ = cache_k.shape
    row = pl.BlockSpec((nq * heads, HEAD_W), lambda s, p, pt: (s, 0))
    pg = pl.BlockSpec((None, page * heads, HEAD_W), lambda s, p, pt: (pt[s, n_pages - 1 - p], 0, 0))
    rows_view = (cache_k.shape[0], page * heads, HEAD_W)
    return pl.pallas_call(
        functools.partial(_sb_sample_body, heads=heads),
        grid_spec=pltpu.PrefetchScalarGridSpec(
            num_scalar_prefetch=1,
            grid=(n_seq, n_pages),
            in_specs=[row, row, row, pg, pg],
            out_specs=row,
            scratch_shapes=[pltpu.VMEM((heads * 16, HEAD_W), F32),
                            pltpu.VMEM((heads * 16, LANES), F32)]),
        out_shape=jax.ShapeDtypeStruct(q2.shape, F32),
        compiler_params=_params(("parallel", "arbitrary")),
        name="sb_sample",
    )(page_table, q2, k2, v2, cache_k.reshape(rows_view), cache_v.reshape(rows_view))


def _moba_sample_body(pt_ref, q_ref, kn_ref, vn_ref, ka_ref, kb_ref, va_ref, vb_ref, o_ref,
                      o_sc, m_sc, l_sc, g_sc, *, heads):
    n = pl.program_id(1)
    page = ka_ref.shape[0] // heads
    nq = q_ref.shape[0] // heads
    pad = 16
    nb = g_sc.shape[0]
    scale = HEAD_W ** -0.5

    def head_rows(ref, h):
        return ref[pl.ds(h, page, stride=heads), :]

    def q_head(h):
        return _pad_rows(q_ref[h * nq:(h + 1) * nq, :], pad)

    gates, scores = [], []
    for h in range(heads):
        qf = q_head(h)
        k = jnp.concatenate([head_rows(ka_ref, h), head_rows(kb_ref, h)], axis=0)
        gates.append(jnp.sum(qf * jnp.mean(k, axis=0, keepdims=True), axis=-1, keepdims=True))
        scores.append(_dot_nt(qf.astype(BF16), k.astype(BF16)) * scale)
    s = jnp.concatenate(scores, axis=0)
    m = jnp.max(s, axis=-1, keepdims=True)
    pr = jnp.exp(s - m)
    g_sc[n] = jnp.concatenate(gates, axis=0)
    m_sc[n] = m
    l_sc[n] = jnp.sum(pr, axis=-1, keepdims=True)
    prb = pr.astype(BF16)
    for h in range(heads):
        v = jnp.concatenate([head_rows(va_ref, h), head_rows(vb_ref, h)], axis=0).astype(BF16)
        o_sc[n, h * pad:(h + 1) * pad, :] = _dot(prb[h * pad:(h + 1) * pad], v)

    @pl.when(n == nb - 1)
    def _():
        n_rows = heads * pad
        n_io = lax.broadcasted_iota(I32, (nb, n_rows, 1), 0)
        gm = g_sc[...]
        sel = jnp.zeros(gm.shape, F32)
        for _ in range(MOBA_TOPK):
            mx = jnp.max(gm, axis=0, keepdims=True)
            idx = jnp.min(jnp.where(gm == mx, n_io, nb), axis=0, keepdims=True)
            hit = n_io == idx
            sel = jnp.where(hit, jnp.where(mx > NEG_INF, 1.0, 0.0), sel)
            gm = jnp.where(hit, NEG_INF, gm)
        s_own = jnp.concatenate(
            [_dot_nt(q_head(h).astype(BF16), _pad_rows(kn_ref[h * nq:(h + 1) * nq, :], LANES).astype(BF16))
             for h in range(heads)], axis=0) * scale
        rr = lax.broadcasted_iota(I32, s_own.shape, 0) & (pad - 1)
        cc = lax.broadcasted_iota(I32, s_own.shape, 1)
        s_own = jnp.where(cc <= rr, s_own, NEG_INF)
        m_own = jnp.max(s_own, axis=-1, keepdims=True)
        pr_own = jnp.exp(s_own - m_own)
        l_own = jnp.sum(pr_own, axis=-1, keepdims=True)
        prb_own = pr_own.astype(BF16)
        o_own = jnp.concatenate(
            [_dot(prb_own[h * pad:(h + 1) * pad], _pad_rows(vn_ref[h * nq:(h + 1) * nq, :], LANES).astype(BF16))
             for h in range(heads)], axis=0)
        m_all = m_sc[...]
        m_tot = jnp.maximum(m_own, jnp.max(jnp.where(sel > 0.0, m_all, NEG_INF), axis=0))
        w = jnp.where(sel > 0.0, jnp.exp(m_all - m_tot), 0.0)
        w_own = jnp.exp(m_own - m_tot)
        l_tot = w_own * l_own + jnp.sum(w * l_sc[...], axis=0)
        o_tot = (w_own * o_own + jnp.sum(w * o_sc[...], axis=0)) / l_tot
        for h in range(heads):
            o_ref[h * nq:(h + 1) * nq, :] = o_tot[h * pad:h * pad + nq]


def _moba_sample(page_table, q2, k2, v2, cache_k, cache_v, nq):
    n_seq, n_pages = page_table.shape
    n_pool, page, heads, _ = cache_k.shape
    assert MOBA_BLOCK == 2 * page and n_pages % 2 == 0 and nq <= 16
    nb = n_pages // 2
    n_rows = heads * 16
    row = pl.BlockSpec((nq * heads, HEAD_W), lambda s, n, pt: (s, 0))
    pg_a = pl.BlockSpec((None, page * heads, HEAD_W), lambda s, n, pt: (pt[s, 2 * n], 0, 0))
    pg_b = pl.BlockSpec((None, page * heads, HEAD_W), lambda s, n, pt: (pt[s, 2 * n + 1], 0, 0))
    ck = cache_k.reshape(n_pool, page * heads, HEAD_W)
    cv = cache_v.reshape(n_pool, page * heads, HEAD_W)
    return pl.pallas_call(
        functools.partial(_moba_sample_body, heads=heads),
        grid_spec=pltpu.PrefetchScalarGridSpec(
            num_scalar_prefetch=1,
            grid=(n_seq, nb),
            in_specs=[row, row, row, pg_a, pg_b, pg_a, pg_b],
            out_specs=row,
            scratch_shapes=[pltpu.VMEM((nb, n_rows, HEAD_W), F32),
                            pltpu.VMEM((nb, n_rows, 1), F32),
                            pltpu.VMEM((nb, n_rows, 1), F32),
                            pltpu.VMEM((nb, n_rows, 1), F32)]),
        out_shape=jax.ShapeDtypeStruct(q2.shape, F32),
        compiler_params=_params(("parallel", "arbitrary")),
        name="moba_sample",
    )(page_table, q2, k2, v2, ck, ck, cv, cv)


def _topk_rows(s, k_top):
    n = s.shape[0]
    row = lax.broadcasted_iota(I32, s.shape, 0)
    rank = jnp.full(s.shape, k_top, I32)
    vals, idxs = [], []
    for k in range(k_top):
        mx = jnp.max(s, axis=0, keepdims=True)
        idx = jnp.min(jnp.where(s == mx, row, n), axis=0, keepdims=True)
        hit = row == idx
        rank = jnp.where(hit, k, rank)
        s = jnp.where(hit, NEG_INF, s)
        vals.append(mx)
        idxs.append(idx)
    return jnp.concatenate(vals, axis=0), jnp.concatenate(idxs, axis=0), rank


def _peer_route_body(x_ref, wq_ref, sk_ref, e0_ref, thr_ref, e1_ref, r1_ref):
    kt = PEER_TOPK
    assert kt == 16, "the candidate-cell list below is written out for a 16 x 16 grid"
    x = x_ref[...]
    k_io = lax.broadcasted_iota(I32, (kt, x.shape[0]), 0)
    for h in range(PEER_HEADS):
        scores, tops = [], []
        for half in range(2):
            col = (h * 2 + half) * PEER_NKEYS
            q = _dot(x, wq_ref[:, col:col + PEER_NKEYS])
            s = _dot_nt(sk_ref[half], q.astype(BF16))
            scores.append(s)
            tops.append(_topk_rows(s, kt))
        (sv0, _, rank0), (sv1, _, rank1) = tops
        cand = jnp.concatenate([sv0[0:1] + sv1] + [sv0[a:a + 1] + sv1[0:8] for a in range(1, 8)]
                               + [sv0[8:16] + sv1[0:1]], axis=0)
        _, ci, _ = _topk_rows(cand, kt)
        cell_a = jnp.where(ci < 16, 0, jnp.where(ci < 72, 1 + ((ci - 16) >> 3), ci - 64))
        count = jnp.zeros(k_io.shape, F32)
        for j in range(kt):
            count = count + jnp.where(k_io == cell_a[j:j + 1], 1.0, 0.0)
        w0 = jnp.exp(sv0 - sv0[0:1])
        w1 = jnp.exp(sv1 - sv1[0:1])
        z = jnp.zeros((1, x.shape[0]), F32)
        for a in range(kt):
            inner = jnp.sum(jnp.where(k_io.astype(F32) < count[a:a + 1], w1, 0.0), axis=0, keepdims=True)
            z = z + w0[a:a + 1] * inner
        thr = jnp.zeros(rank0.shape, F32)
        for a in range(kt):
            thr = jnp.where(rank0 == a, count[a:a + 1], thr)
        e0_ref[h] = jnp.where(rank0 < kt, jnp.exp(scores[0] - sv0[0:1]), 0.0) / z
        thr_ref[h] = thr
        e1_ref[h] = jnp.where(rank1 < kt, jnp.exp(scores[1] - sv1[0:1]), 0.0)
        r1_ref[h] = rank1.astype(F32)


def _peer_route(xb, wqb, skb):
    m, d = xb.shape
    tb = TOKEN_TILE
    spec = pl.BlockSpec((PEER_HEADS, PEER_NKEYS, tb), lambda i: (0, 0, i))
    shape = jax.ShapeDtypeStruct((PEER_HEADS, PEER_NKEYS, m), F32)
    return pl.pallas_call(
        _peer_route_body,
        grid=(m // tb,),
        in_specs=[pl.BlockSpec((tb, d), lambda i: (i, 0)),
                  pl.BlockSpec(wqb.shape, lambda i: (0, 0)),
                  pl.BlockSpec(skb.shape, lambda i: (0, 0, 0))],
        out_specs=[spec] * 4,
        out_shape=[shape] * 4,
        compiler_params=_params(("parallel",)),
        name="peer_route",
    )(xb, wqb, skb)


def _gelu(x):
    return 0.5 * x * (1.0 + lax.erf(x * (2.0 ** -0.5)))


def _peer_expert_body(x_ref, e0_ref, thr_ref, e1_ref, r1_ref, u_ref, vt_ref, r_ref, g_ref, b_ref,
                      of_ref, ob_ref, acc_ref):
    c = pl.program_id(1)
    rows_per_chunk = u_ref.shape[0] // PEER_NKEYS

    @pl.when(c == 0)
    def _():
        acc_ref[...] = jnp.zeros(acc_ref.shape, F32)

    x = x_ref[...]
    parts = []
    for rr in range(rows_per_chunk):
        i0 = c * rows_per_chunk + rr
        hid = _dot_nt(u_ref[rr * PEER_NKEYS:(rr + 1) * PEER_NKEYS, :], x)
        thr_rows = [thr_ref[h, pl.ds(i0, 1), :] for h in range(PEER_HEADS)]
        e0_rows = [e0_ref[h, pl.ds(i0, 1), :] for h in range(PEER_HEADS)]
        halves = []
        for lt in range(x.shape[0] // LANES):
            cols = slice(lt * LANES, (lt + 1) * LANES)
            w = jnp.zeros((PEER_NKEYS, LANES), F32)
            for h in range(PEER_HEADS):
                keep = r1_ref[h, :, cols] < thr_rows[h][:, cols]
                w = w + jnp.where(keep, e1_ref[h, :, cols], 0.0) * e0_rows[h][:, cols]
            halves.append((_gelu(hid[:, cols]) * w).astype(BF16))
        parts.append(jnp.concatenate(halves, axis=1))
    aw = jnp.concatenate(parts, axis=0)
    acc_ref[...] += _dot(vt_ref[...], aw)

    @pl.when(c == pl.num_programs(1) - 1)
    def _():
        y = DN_ALPHA * r_ref[...] + acc_ref[...].T
        o = _layernorm_rows(y, g_ref[...], b_ref[...])
        of_ref[...] = o
        ob_ref[...] = o.astype(BF16)


def _peer_expert(xb, routes, ub, vtb, resid, g, b):
    m, d = xb.shape
    n_exp = ub.shape[0]
    tb = TOKEN_TILE
    ec = PEER_CHUNK
    rspec = pl.BlockSpec((PEER_HEADS, PEER_NKEYS, tb), lambda i, c: (0, 0, i))
    return pl.pallas_call(
        _peer_expert_body,
        grid=(m // tb, n_exp // ec),
        in_specs=[pl.BlockSpec((tb, d), lambda i, c: (i, 0)),
                  rspec, rspec, rspec, rspec,
                  pl.BlockSpec((ec, d), lambda i, c: (c, 0)),
                  pl.BlockSpec((d, ec), lambda i, c: (0, c)),
                  pl.BlockSpec((tb, d), lambda i, c: (i, 0)),
                  pl.BlockSpec((1, d), lambda i, c: (0, 0)),
                  pl.BlockSpec((1, d), lambda i, c: (0, 0))],
        out_specs=[pl.BlockSpec((tb, d), lambda i, c: (i, 0)), pl.BlockSpec((tb, d), lambda i, c: (i, 0))],
        out_shape=[jax.ShapeDtypeStruct((m, d), F32), jax.ShapeDtypeStruct((m, d), BF16)],
        scratch_shapes=[pltpu.VMEM((d, tb), F32)],
        compiler_params=_params(("parallel", "arbitrary")),
        name="peer_expert",
    )(xb, *routes, ub, vtb, resid, g.reshape(1, d), b.reshape(1, d))


def _peer_ln(h, hb, wqb, skb, ub, vtb, g, b):
    routes = _peer_route(hb, wqb, skb)
    return _peer_expert(hb, routes, ub, vtb, h, g, b)


def _token_head_rows(x):
    return x.reshape(-1, HEAD_W)


def _head_token_rows(x, n_seq):
    tokens = x.shape[0] // n_seq
    heads = x.shape[1] // HEAD_W
    return x.reshape(n_seq, tokens, heads, HEAD_W).transpose(0, 2, 1, 3).reshape(-1, HEAD_W)


def _rope_tables(pos, width):
    half = width // 2
    inv = ROPE_THETA ** (-jnp.arange(half, dtype=F32) / half)
    ang = pos.astype(F32)[:, None] * inv[None, :]
    cos = jnp.tile(jnp.cos(ang), (1, LANES // half))
    sin = jnp.sin(ang)
    sin = jnp.tile(jnp.concatenate([-sin, sin], axis=1), (1, LANES // width))
    return cos, sin


def kernel(x_prompt, x_sample, cache_diff_k, cache_diff_v, cache_sb_k, cache_sb_v, cache_moba_k,
           cache_moba_v, page_table, even_w_in, even_w_out, diff_lambda_q1, diff_lambda_k1,
           diff_lambda_q2, diff_lambda_k2, diff_subln_g, odd_w_in, odd_w_out, ln_mix_g, ln_mix_b,
           ln_ffn_g, ln_ffn_b, peer_w_q, peer_subkeys, peer_u, peer_v):
    batch, seq, d = x_prompt.shape
    n_seq, nq, _ = x_sample.shape
    page = cache_diff_k.shape[2]
    past_len = page_table.shape[1] * page
    depth = ln_mix_g.shape[0]
    assert depth == DEPTH and seq % ATTN_TILE == 0 and past_len % MOBA_BLOCK == 0
    assert (batch * seq) % TOKEN_TILE == 0 and (n_seq * nq) % TOKEN_TILE == 0 and nq == 8

    pos_p = jnp.tile(jnp.arange(seq, dtype=I32), batch)
    pos_s = jnp.tile(past_len + jnp.arange(nq, dtype=I32), n_seq)
    groups = [
        dict(h=x_prompt.reshape(batch * seq, d), pos=pos_p),
        dict(h=x_sample.reshape(n_seq * nq, d), pos=pos_s),
    ]
    for gr in groups:
        gr["hb"] = gr["h"].astype(BF16)
        gr["rope64"] = _rope_tables(gr["pos"], DIFF_DK)
        gr["rope128"] = _rope_tables(gr["pos"], HEAD_W)
    new_rows = [dict(), dict()]

    for layer in range(depth):
        li = layer // 2
        mix = []
        if layer % 2 == 0:
            lam_init = 0.8 - 0.6 * math.exp(-0.3 * layer)
            lam4 = jnp.stack([diff_lambda_q1[li], diff_lambda_k1[li], diff_lambda_q2[li], diff_lambda_k2[li]])
            sub_g = diff_subln_g[li].reshape(1, HEAD_W)
            w_in = even_w_in[li].astype(BF16)
            hw = DIFF_HEADS * HEAD_W
            w_cols = [w_in[:, j * hw:(j + 1) * hw] for j in range(6)]
            w_out = even_w_out[li].astype(BF16)
            ck = cache_diff_k[li].reshape(-1, page, DIFF_HEADS, HEAD_W)
            cv, sk, sv = cache_diff_v[li], cache_sb_k[li], cache_sb_v[li]
            for gi, gr in enumerate(groups):
                cos, sin = gr["rope64"]
                qd_f, qd_b = _proj(gr["hb"], w_cols[0], cos, sin, rope=DIFF_DK)
                kd_f, kd_b = _proj(gr["hb"], w_cols[1], cos, sin, rope=DIFF_DK)
                vd_f, vd_b = _proj(gr["hb"], w_cols[2], cos, sin)
                qs_f, qs_b = _proj(gr["hb"], w_cols[3], cos, sin)
                ks_f, ks_b = _proj(gr["hb"], w_cols[4], cos, sin)
                vs_f, vs_b = _proj(gr["hb"], w_cols[5], cos, sin)
                new_rows[gi].update(dk=kd_f, dv=vd_f, sk=ks_f, sv=vs_f)
                if gi == 0:
                    od = _diff_prompt(qd_b, kd_b, vd_b, lam4, sub_g, batch, seq, lam_init)
                    osb = _sb_prompt(qs_b, ks_b, vs_b, batch, seq)
                else:
                    od = _diff_sample(page_table, lam4, sub_g, _token_head_rows(qd_f), _token_head_rows(kd_f),
                                      _token_head_rows(vd_f), ck, cv, nq, lam_init).reshape(-1, hw)
                    osb = _sb_sample(page_table, _head_token_rows(qs_f, n_seq), _head_token_rows(ks_f, n_seq),
                                     _head_token_rows(vs_f, n_seq), sk, sv, nq)
                    osb = osb.reshape(n_seq, SB_HEADS, nq, HEAD_W).transpose(0, 2, 1, 3).reshape(-1, hw)
                mix.append([od, osb])
        else:
            w_in = odd_w_in[li].astype(BF16)
            hw = MOBA_HEADS * HEAD_W
            w_cols = [w_in[:, j * hw:(j + 1) * hw] for j in range(3)]
            w_out = odd_w_out[li].astype(BF16)
            mk, mv = cache_moba_k[li], cache_moba_v[li]
            for gi, gr in enumerate(groups):
                cos, sin = gr["rope128"]
                q_f, q_b = _proj(gr["hb"], w_cols[0], cos, sin, rope=HEAD_W)
                if gi == 0:
                    k_f, k_b, k_mean = _proj(gr["hb"], w_cols[1], cos, sin, rope=HEAD_W, want_mean=True)
                else:
                    k_f, k_b = _proj(gr["hb"], w_cols[1], cos, sin, rope=HEAD_W)
                v_f, v_b = _proj(gr["hb"], w_cols[2], cos, sin)
                new_rows[gi].update(mk=k_f, mv=v_f)
                if gi == 0:
                    k_mean = k_mean.reshape(batch, seq // MOBA_BLOCK, hw)
                    o = _moba_prompt(q_f, q_b, k_mean, k_b, v_b, batch, seq)
                else:
                    o = _moba_sample(page_table, _head_token_rows(q_f, n_seq), _head_token_rows(k_f, n_seq),
                                     _head_token_rows(v_f, n_seq), mk, mv, nq)
                    o = o.reshape(n_seq, MOBA_HEADS, nq, HEAD_W).transpose(0, 2, 1, 3).reshape(-1, hw)
                mix.append([o])
        wqb = peer_w_q[layer].astype(BF16)
        skb = peer_subkeys[layer].astype(BF16)
        ub = peer_u[layer].astype(BF16)
        vtb = peer_v[layer].astype(BF16).T
        for gi, gr in enumerate(groups):
            h, hb = _outproj_ln(mix[gi], w_out, gr["h"], ln_mix_g[layer], ln_mix_b[layer])
            gr["h"], gr["hb"] = _peer_ln(h, hb, wqb, skb, ub, vtb, ln_ffn_g[layer], ln_ffn_b[layer])

    def rows(gi, key, lead, tail):
        return new_rows[gi][key].reshape((1,) + lead + tail)

    lp, ls = (batch, seq), (n_seq, nq)
    t_dk = (DIFF_HEADS, 2, DIFF_DK)
    t_h8 = (DIFF_HEADS, HEAD_W)
    t_h16 = (MOBA_HEADS, HEAD_W)
    return (groups[0]["h"].reshape(batch, seq, d), groups[1]["h"].reshape(n_seq, nq, d),
            rows(0, "dk", lp, t_dk), rows(0, "dv", lp, t_h8), rows(0, "sk", lp, t_h8), rows(0, "sv", lp, t_h8),
            rows(0, "mk", lp, t_h16), rows(0, "mv", lp, t_h16),
            rows(1, "dk", ls, t_dk), rows(1, "dv", ls, t_h8), rows(1, "sk", ls, t_h8), rows(1, "sv", ls, t_h8),
            rows(1, "mk", ls, t_h16), rows(1, "mv", ls, t_h16))
```

```python
import functools
import math

import jax
import jax.numpy as jnp
from jax import lax
from jax.experimental import pallas as pl
from jax.experimental.pallas import tpu as pltpu

F32 = jnp.float32
BF16 = jnp.bfloat16
I32 = jnp.int32

LANES = 128
DIFF_HEADS = 8
DIFF_DK = 64
SB_HEADS = 8
MOBA_HEADS = 16
HEAD_W = 128
MOBA_BLOCK = 256
MOBA_TOPK = 3
ROPE_THETA = 10000.0
PEER_HEADS = 8
PEER_NKEYS = 128
PEER_TOPK = 16
DEPTH = 2
DN_ALPHA = (2 * DEPTH) ** 0.25
LN_EPS = 1e-5
SUBLN_EPS = 1e-5
NEG_INF = float("-inf")

TOKEN_TILE = 256
ATTN_TILE = 256
KEY_TILE = 512
PEER_CHUNK = 1024
SAMPLE_PAGE_GROUP = 4
VMEM_LIMIT = 56 * 1024 * 1024

_NT = (((1,), (1,)), ((), ()))


def _params(sem):
    return pltpu.CompilerParams(dimension_semantics=sem, vmem_limit_bytes=VMEM_LIMIT)


def _dot(a, b):
    return jnp.dot(a, b, preferred_element_type=F32)


def _dot_nt(a, b):
    return lax.dot_general(a, b, _NT, preferred_element_type=F32)


def _split_bf16(x):
    hi = x.astype(BF16)
    lo = (x - hi.astype(F32)).astype(BF16)
    return hi, lo


def _proj_body(x_ref, w_ref, cos_ref, sin_ref, *out_refs, rope, want_mean):
    of_ref, ob_ref = out_refs[0], out_refs[1]
    n = w_ref.shape[1]
    x = x_ref[...]
    for c in range(n // 256):
        acc = _dot(x, w_ref[:, c * 256:(c + 1) * 256])
        if rope:
            parts = []
            for hh in range(2):
                y = acc[:, hh * LANES:(hh + 1) * LANES]
                if rope == LANES:
                    sw = pltpu.roll(y, LANES // 2, 1)
                else:
                    lane = lax.broadcasted_iota(I32, y.shape, 1)
                    first = (lane % rope) < rope // 2
                    sw = jnp.where(first, pltpu.roll(y, LANES - rope // 2, 1),
                                   pltpu.roll(y, rope // 2, 1))
                parts.append(y * cos_ref[...] + sw * sin_ref[...])
            acc = jnp.concatenate(parts, axis=1)
        of_ref[:, c * 256:(c + 1) * 256] = acc
        ob_ref[:, c * 256:(c + 1) * 256] = acc.astype(BF16)
        if want_mean:
            out_refs[2][0, :, c * 256:(c + 1) * 256] = jnp.mean(acc, axis=0, keepdims=True)


def _proj(xb, wb, cos, sin, rope=0, want_mean=False):
    m, k = xb.shape
    n = wb.shape[1]
    tm = TOKEN_TILE
    out_shape = [jax.ShapeDtypeStruct((m, n), F32), jax.ShapeDtypeStruct((m, n), BF16)]
    out_specs = [pl.BlockSpec((tm, n), lambda i: (i, 0)), pl.BlockSpec((tm, n), lambda i: (i, 0))]
    if want_mean:
        out_shape.append(jax.ShapeDtypeStruct((m // tm, 1, n), F32))
        out_specs.append(pl.BlockSpec((1, 1, n), lambda i: (i, 0, 0)))
    return pl.pallas_call(
        functools.partial(_proj_body, rope=rope, want_mean=want_mean),
        grid=(m // tm,),
        in_specs=[pl.BlockSpec((tm, k), lambda i: (i, 0)),
                  pl.BlockSpec((k, n), lambda i: (0, 0)),
                  pl.BlockSpec((tm, LANES), lambda i: (i, 0)),
                  pl.BlockSpec((tm, LANES), lambda i: (i, 0))],
        out_specs=out_specs,
        out_shape=out_shape,
        compiler_params=_params(("parallel",)),
        name="proj",
    )(xb, wb, cos, sin)


def _layernorm_rows(y, g, b):
    mu = jnp.mean(y, axis=-1, keepdims=True)
    d = y - mu
    var = jnp.mean(d * d, axis=-1, keepdims=True)
    return d * lax.rsqrt(var + LN_EPS) * g + b


def _outproj_ln_body(*refs, n_in):
    a_refs = refs[:n_in]
    w_ref, r_ref, g_ref, b_ref, of_ref, ob_ref = refs[n_in:]
    y = DN_ALPHA * r_ref[...]
    off = 0
    for a_ref in a_refs:
        kk = a_ref.shape[1]
        y = y + _dot(a_ref[...].astype(BF16), w_ref[off:off + kk, :])
        off += kk
    o = _layernorm_rows(y, g_ref[...], b_ref[...])
    of_ref[...] = o
    ob_ref[...] = o.astype(BF16)


def _outproj_ln(a_list, wb, resid, g, b):
    m, d = resid.shape
    tm = TOKEN_TILE
    in_specs = [pl.BlockSpec((tm, a.shape[1]), lambda i: (i, 0)) for a in a_list]
    in_specs += [pl.BlockSpec(wb.shape, lambda i: (0, 0)),
                 pl.BlockSpec((tm, d), lambda i: (i, 0)),
                 pl.BlockSpec((1, d), lambda i: (0, 0)),
                 pl.BlockSpec((1, d), lambda i: (0, 0))]
    return pl.pallas_call(
        functools.partial(_outproj_ln_body, n_in=len(a_list)),
        grid=(m // tm,),
        in_specs=in_specs,
        out_specs=[pl.BlockSpec((tm, d), lambda i: (i, 0)), pl.BlockSpec((tm, d), lambda i: (i, 0))],
        out_shape=[jax.ShapeDtypeStruct((m, d), F32), jax.ShapeDtypeStruct((m, d), BF16)],
        compiler_params=_params(("parallel",)),
        name="outproj_ln",
    )(*a_list, wb, resid, g.reshape(1, d), b.reshape(1, d))


def _with_ones(v):
    return jnp.concatenate([v, jnp.ones((v.shape[0], LANES), v.dtype)], axis=1)


def _softmax_update_wide(s, v_ones, m_ref, acc_ref):
    m_prev = m_ref[...]
    m_new = jnp.maximum(m_prev, jnp.max(s, axis=-1, keepdims=True))
    alpha = jnp.exp(m_prev - m_new)
    p = jnp.concatenate([jnp.exp(s[:, c * LANES:(c + 1) * LANES] - m_new)
                         for c in range(s.shape[1] // LANES)], axis=1)
    acc_ref[...] = jnp.concatenate([alpha, alpha], axis=1) * acc_ref[...] + _dot(p.astype(BF16), v_ones)
    m_ref[...] = m_new


def _stack_diff_q(q):
    lane = lax.broadcasted_iota(I32, q.shape, 1)
    zero = jnp.zeros_like(q)
    return jnp.concatenate([jnp.where(lane < DIFF_DK, q, zero),
                            jnp.where(lane >= DIFF_DK, q, zero)], axis=0)


def _diff_finish(acc, rows, lam_ref, g_ref, lam_init):
    o0 = acc[:rows, :LANES] / acc[:rows, LANES:]
    o1 = acc[rows:, :LANES] / acc[rows:, LANES:]
    lam = (jnp.exp(jnp.sum(lam_ref[0:1, :] * lam_ref[1:2, :], axis=-1, keepdims=True))
           - jnp.exp(jnp.sum(lam_ref[2:3, :] * lam_ref[3:4, :], axis=-1, keepdims=True))
           + lam_init)
    o = o0 - lam * o1
    o = o * lax.rsqrt(jnp.mean(o * o, axis=-1, keepdims=True) + SUBLN_EPS)
    return o * g_ref[...] * (1.0 - lam_init)


def _strict_upper_ones(n):
    r = lax.broadcasted_iota(I32, (n, n + LANES), 0)
    c = lax.broadcasted_iota(I32, (n, n + LANES), 1)
    return jnp.where(r > c, 1.0, jnp.where(c >= n, 1.0, 0.0)).astype(BF16)


def _stick_break_weights(z, tri, past, c_ref):
    t = jnp.log(1.0 + jnp.exp(-jnp.abs(z)))
    log_beta = jnp.minimum(z, 0.0) - t
    log_keep = -jnp.maximum(z, 0.0) - t
    if past is not None:
        log_keep = jnp.where(past, log_keep, 0.0)
    hi, lo = _split_bf16(log_keep)
    rows, keys = z.shape
    parts = _dot(jnp.concatenate([hi, lo], axis=0), tri)
    sums = parts[:rows] + parts[rows:]
    carry = c_ref[...]
    a = jnp.concatenate([jnp.exp(log_beta[:, c * LANES:(c + 1) * LANES]
                                 + sums[:, c * LANES:(c + 1) * LANES] + carry)
                         for c in range(keys // LANES)], axis=1)
    if past is not None:
        a = jnp.where(past, a, 0.0)
    c_ref[...] = carry + sums[:, keys:]
    return a.astype(BF16)


def _diff_prompt_body(lam_ref, g_ref, q_ref, k_ref, v_ref, o_ref, acc_ref, m_ref, *, lam_init):
    tq = q_ref.shape[0]
    tk = min(KEY_TILE, k_ref.shape[0])
    i = pl.program_id(2)
    qz = _stack_diff_q(q_ref[...])
    m_ref[...] = jnp.full(m_ref.shape, NEG_INF, F32)
    acc_ref[...] = jnp.zeros(acc_ref.shape, F32)
    scale = DIFF_DK ** -0.5
    n_full = (i * tq) // tk

    def step(j, diagonal):
        rows = pl.ds(pl.multiple_of(j * tk, tk), tk)
        s = _dot_nt(qz, k_ref[rows, :]) * scale
        if diagonal:
            r = lax.broadcasted_iota(I32, s.shape, 0) % tq + i * tq
            c = lax.broadcasted_iota(I32, s.shape, 1) + j * tk
            s = jnp.where(c <= r, s, NEG_INF)
        _softmax_update_wide(s, _with_ones(v_ref[rows, :]), m_ref, acc_ref)

    def body(j, carry):
        step(j, False)
        return carry

    lax.fori_loop(0, n_full, body, 0)
    step(n_full, True)
    o = _diff_finish(acc_ref[...], tq, lam_ref, g_ref, lam_init)
    o_ref[...] = o.astype(BF16)


def _diff_prompt(qb, kb, vb, lam4, g, batch, seq, lam_init):
    tq = ATTN_TILE
    nq = seq // tq
    return pl.pallas_call(
        functools.partial(_diff_prompt_body, lam_init=lam_init),
        grid=(batch, DIFF_HEADS, nq),
        in_specs=[pl.BlockSpec((4, DIFF_DK), lambda b, h, i: (0, 0)),
                  pl.BlockSpec((1, HEAD_W), lambda b, h, i: (0, 0)),
                  pl.BlockSpec((tq, HEAD_W), lambda b, h, i: (b * nq + i, h)),
                  pl.BlockSpec((seq, HEAD_W), lambda b, h, i: (b, h)),
                  pl.BlockSpec((seq, HEAD_W), lambda b, h, i: (b, h))],
        out_specs=pl.BlockSpec((tq, HEAD_W), lambda b, h, i: (b * nq + i, h)),
        out_shape=jax.ShapeDtypeStruct(qb.shape, BF16),
        scratch_shapes=[pltpu.VMEM((2 * tq, 2 * LANES), F32),
                        pltpu.VMEM((2 * tq, LANES), F32)],
        compiler_params=_params(("parallel", "parallel", "arbitrary")),
        name="diff_prompt",
    )(lam4, g, qb, kb, vb)


def _sb_prompt_body(q_ref, k_ref, v_ref, o_ref, acc_ref, c_ref):
    tq = q_ref.shape[0]
    tk = min(KEY_TILE, k_ref.shape[0])
    i = pl.program_id(2)
    q = q_ref[...]
    tri = _strict_upper_ones(tk)
    acc_ref[...] = jnp.zeros(acc_ref.shape, F32)
    c_ref[...] = jnp.zeros(c_ref.shape, F32)
    scale = HEAD_W ** -0.5
    n_full = (i * tq) // tk

    def step(j, diagonal):
        rows = pl.ds(pl.multiple_of(j * tk, tk), tk)
        z = _dot_nt(q, k_ref[rows, :]) * scale
        past = None
        if diagonal:
            r = lax.broadcasted_iota(I32, z.shape, 0) + i * tq
            c = lax.broadcasted_iota(I32, z.shape, 1) + j * tk
            past = c < r
        a = _stick_break_weights(z, tri, past, c_ref)
        acc_ref[...] += _dot(a, v_ref[rows, :])

    step(n_full, True)

    def body(jj, carry):
        step(n_full - 1 - jj, False)
        return carry

    lax.fori_loop(0, n_full, body, 0)
    o_ref[...] = acc_ref[...].astype(BF16)


def _sb_prompt(qb, kb, vb, batch, seq):
    tq = ATTN_TILE
    nq = seq // tq
    return pl.pallas_call(
        _sb_prompt_body,
        grid=(batch, SB_HEADS, nq),
        in_specs=[pl.BlockSpec((tq, HEAD_W), lambda b, h, i: (b * nq + i, h)),
                  pl.BlockSpec((seq, HEAD_W), lambda b, h, i: (b, h)),
                  pl.BlockSpec((seq, HEAD_W), lambda b, h, i: (b, h))],
        out_specs=pl.BlockSpec((tq, HEAD_W), lambda b, h, i: (b * nq + i, h)),
        out_shape=jax.ShapeDtypeStruct(qb.shape, BF16),
        scratch_shapes=[pltpu.VMEM((tq, HEAD_W), F32), pltpu.VMEM((tq, LANES), F32)],
        compiler_params=_params(("parallel", "parallel", "arbitrary")),
        name="sb_prompt",
    )(qb, kb, vb)


def _moba_prompt_body(qf_ref, qb_ref, km_ref, k_ref, v_ref, o_ref, acc_ref, m_ref):
    tq = qb_ref.shape[0]
    tk = min(KEY_TILE, k_ref.shape[0])
    nb = km_ref.shape[0]
    i = pl.program_id(2)
    qb = qb_ref[...]
    scale = HEAD_W ** -0.5

    qh, ql = _split_bf16(qf_ref[...])
    kh, kl = _split_bf16(km_ref[...])
    gate = _dot_nt(kh, qh) + _dot_nt(kl, qh) + _dot_nt(kh, ql)
    n_io = lax.broadcasted_iota(I32, gate.shape, 0)
    gm = jnp.where(n_io < i, gate, NEG_INF)
    rank = jnp.zeros(gate.shape, I32)
    for n2 in range(nb):
        row = gm[n2:n2 + 1, :]
        beats = jnp.where(row > gm, 1, jnp.where(row == gm, jnp.where(n2 < n_io, 1, 0), 0))
        rank = rank + beats
    sel_t = jnp.where(n_io < i, jnp.where(rank < MOBA_TOPK, 1.0, 0.0), jnp.where(n_io == i, 1.0, 0.0))
    sel = _pad_rows(sel_t, LANES).T.astype(BF16)

    m_ref[...] = jnp.full(m_ref.shape, NEG_INF, F32)
    acc_ref[...] = jnp.zeros(acc_ref.shape, F32)
    blocks_per_tile = tk // tq
    own_tile = i // blocks_per_tile
    shift = tq.bit_length() - 1
    r = lax.broadcasted_iota(I32, (tq, tk), 0)
    c = lax.broadcasted_iota(I32, (tq, tk), 1)
    blk_row = lax.broadcasted_iota(I32, (LANES, tk), 0)
    blk_col = lax.broadcasted_iota(I32, (LANES, tk), 1) >> shift

    def step(j, has_own):
        rows = pl.ds(pl.multiple_of(j * tk, tk), tk)
        s = _dot_nt(qb, k_ref[rows, :]) * scale
        expand = jnp.where(blk_row == blk_col + j * blocks_per_tile, 1.0, 0.0).astype(BF16)
        s = jnp.where(_dot(sel, expand) > 0.5, s, NEG_INF)
        if has_own:
            own_block = (c >> shift) + j * blocks_per_tile == i
            s = jnp.where(jnp.logical_or((c & (tq - 1)) <= r, jnp.logical_not(own_block)), s, NEG_INF)
        _softmax_update_wide(s, _with_ones(v_ref[rows, :]), m_ref, acc_ref)

    step(own_tile, True)

    def body(j, carry):
        step(j, False)
        return carry

    lax.fori_loop(0, own_tile, body, 0)
    acc = acc_ref[...]
    o_ref[...] = (acc[:, :LANES] / acc[:, LANES:]).astype(BF16)


def _moba_prompt(qf, qb, kmean, kb, vb, batch, seq):
    tq = MOBA_BLOCK
    nq = seq // tq
    return pl.pallas_call(
        _moba_prompt_body,
        grid=(batch, MOBA_HEADS, nq),
        in_specs=[pl.BlockSpec((tq, HEAD_W), lambda b, h, i: (b * nq + i, h)),
                  pl.BlockSpec((tq, HEAD_W), lambda b, h, i: (b * nq + i, h)),
                  pl.BlockSpec((None, nq, HEAD_W), lambda b, h, i: (b, 0, h)),
                  pl.BlockSpec((seq, HEAD_W), lambda b, h, i: (b, h)),
                  pl.BlockSpec((seq, HEAD_W), lambda b, h, i: (b, h))],
        out_specs=pl.BlockSpec((tq, HEAD_W), lambda b, h, i: (b * nq + i, h)),
        out_shape=jax.ShapeDtypeStruct(qb.shape, BF16),
        scratch_shapes=[pltpu.VMEM((tq, 2 * LANES), F32),
                        pltpu.VMEM((tq, LANES), F32)],
        compiler_params=_params(("parallel", "parallel", "arbitrary")),
        name="moba_prompt",
    )(qf, qb, kmean, kb, vb)


def _pad_rows(x, n):
    return jnp.concatenate([x, jnp.zeros((n - x.shape[0],) + x.shape[1:], x.dtype)], axis=0)


def _same_head(shape, heads):
    rr = lax.broadcasted_iota(I32, shape, 0)
    cc = lax.broadcasted_iota(I32, shape, 1)
    return (rr & (heads - 1)) == (cc & (heads - 1)), rr, cc


def _page_matrix(ref):
    rows, heads, width = ref.shape
    return ref[...].reshape(rows * heads, width).astype(BF16)


def _diff_sample_body(pt_ref, lam_ref, g_ref, q_ref, kn_ref, vn_ref, *refs, lam_init, group):
    kp_refs, vp_refs = refs[:group], refs[group:2 * group]
    o_ref, acc_ref, m_ref = refs[2 * group:]
    p = pl.program_id(1)
    page, heads, _ = kp_refs[0].shape
    n_rows = q_ref.shape[0]
    shift = heads.bit_length() - 1
    scale = DIFF_DK ** -0.5
    qz = _stack_diff_q(q_ref[...]).astype(BF16)

    @pl.when(p == 0)
    def _():
        m_ref[...] = jnp.full(m_ref.shape, NEG_INF, F32)
        acc_ref[...] = jnp.zeros(acc_ref.shape, F32)

    def update(k2d, v2d, causal):
        s = _dot_nt(qz, k2d) * scale
        ok, rr, cc = _same_head(s.shape, heads)
        s = jnp.where(ok, s, NEG_INF)
        if causal:
            query = (rr & (n_rows - 1)) >> shift
            s = jnp.where((cc >> shift) <= query, s, NEG_INF)
        _softmax_update_wide(s, _with_ones(v2d), m_ref, acc_ref)

    update(jnp.concatenate([_page_matrix(r) for r in kp_refs], axis=0),
           jnp.concatenate([_page_matrix(r) for r in vp_refs], axis=0), False)

    @pl.when(p == pl.num_programs(1) - 1)
    def _():
        update(_pad_rows(kn_ref[...], 2 * n_rows).astype(BF16),
               _pad_rows(vn_ref[...], 2 * n_rows).astype(BF16), True)
        o_ref[...] = _diff_finish(acc_ref[...], n_rows, lam_ref, g_ref, lam_init)


def _page_spec(page, heads, page_of_step):
    return pl.BlockSpec((None, page, heads, HEAD_W), lambda s, p, pt: (pt[s, page_of_step(p)], 0, 0, 0))


def _diff_sample(page_table, lam4, g, q2, k2, v2, cache_k, cache_v, nq, lam_init):
    n_seq, n_pages = page_table.shape
    _, page, heads, _ = cache_k.shape
    n_rows = nq * heads
    group = SAMPLE_PAGE_GROUP
    assert n_rows & (n_rows - 1) == 0 and heads & (heads - 1) == 0 and n_pages % group == 0
    row = pl.BlockSpec((n_rows, HEAD_W), lambda s, p, pt: (s, 0))
    pages = [_page_spec(page, heads, lambda p, j=j: group * p + j) for j in range(group)]
    return pl.pallas_call(
        functools.partial(_diff_sample_body, lam_init=lam_init, group=group),
        grid_spec=pltpu.PrefetchScalarGridSpec(
            num_scalar_prefetch=1,
            grid=(n_seq, n_pages // group),
            in_specs=[pl.BlockSpec((4, DIFF_DK), lambda s, p, pt: (0, 0)),
                      pl.BlockSpec((1, HEAD_W), lambda s, p, pt: (0, 0)),
                      row, row, row] + pages + pages,
            out_specs=row,
            scratch_shapes=[pltpu.VMEM((2 * n_rows, 2 * LANES), F32),
                            pltpu.VMEM((2 * n_rows, LANES), F32)]),
        out_shape=jax.ShapeDtypeStruct(q2.shape, F32),
        compiler_params=_params(("parallel", "arbitrary")),
        name="diff_sample",
    )(page_table, lam4, g, q2, k2, v2, *([cache_k] * group), *([cache_v] * group))


def _sb_sample_body(pt_ref, q_ref, kn_ref, vn_ref, *refs, heads, group):
    kp_refs, vp_refs = refs[:group], refs[group:2 * group]
    o_ref, acc_ref, c_ref = refs[2 * group:]
    p = pl.program_id(1)
    page = kp_refs[0].shape[0] // heads
    nq = q_ref.shape[0] // heads
    pad = 16
    scale = HEAD_W ** -0.5

    def head_rows(page_refs, h):
        return jnp.concatenate([r[pl.ds(h, page, stride=heads), :] for r in page_refs], axis=0)

    def update(get_k, get_v, tri, past):
        zs = []
        for h in range(heads):
            q = _pad_rows(q_ref[h * nq:(h + 1) * nq, :], pad).astype(BF16)
            zs.append(_dot_nt(q, get_k(h).astype(BF16)) * scale)
        a = _stick_break_weights(jnp.concatenate(zs, axis=0), tri, past, c_ref)
        for h in range(heads):
            rows = slice(h * pad, (h + 1) * pad)
            acc_ref[rows, :] += _dot(a[rows], get_v(h).astype(BF16))

    @pl.when(p == 0)
    def _():
        acc_ref[...] = jnp.zeros(acc_ref.shape, F32)
        c_ref[...] = jnp.zeros(c_ref.shape, F32)
        r = lax.broadcasted_iota(I32, (heads * pad, page), 0) & (pad - 1)
        c = lax.broadcasted_iota(I32, (heads * pad, page), 1)
        update(lambda h: _pad_rows(kn_ref[h * nq:(h + 1) * nq, :], page),
               lambda h: _pad_rows(vn_ref[h * nq:(h + 1) * nq, :], page), _strict_upper_ones(page), c < r)

    update(lambda h: head_rows(kp_refs, h), lambda h: head_rows(vp_refs, h),
           _strict_upper_ones(group * page), None)

    @pl.when(p == pl.num_programs(1) - 1)
    def _():
        for h in range(heads):
            o_ref[h * nq:(h + 1) * nq, :] = acc_ref[h * pad:h * pad + nq, :]


def _sb_sample(page_table, q2, k2, v2, cache_k, cache_v, nq):
    n_seq, n_pages = page_table.shape
    _, page, heads, _ = cache_k.shape
    group = SAMPLE_PAGE_GROUP
    assert n_pages % group == 0
    row = pl.BlockSpec((nq * heads, HEAD_W), lambda s, p, pt: (s, 0))
    pages = [pl.BlockSpec((None, page * heads, HEAD_W),
                          lambda s, p, pt, g=g: (pt[s, n_pages - group * (p + 1) + g], 0, 0))
             for g in range(group)]
    rows_view = (cache_k.shape[0], page * heads, HEAD_W)
    return pl.pallas_call(
        functools.partial(_sb_sample_body, heads=heads, group=group),
        grid_spec=pltpu.PrefetchScalarGridSpec(
            num_scalar_prefetch=1,
            grid=(n_seq, n_pages // group),
            in_specs=[row, row, row] + pages + pages,
            out_specs=row,
            scratch_shapes=[pltpu.VMEM((heads * 16, HEAD_W), F32),
                            pltpu.VMEM((heads * 16, LANES), F32)]),
        out_shape=jax.ShapeDtypeStruct(q2.shape, F32),
        compiler_params=_params(("parallel", "arbitrary")),
        name="sb_sample",
    )(page_table, q2, k2, v2, *([cache_k.reshape(rows_view)] * group), *([cache_v.reshape(rows_view)] * group))


def _moba_sample_body(pt_ref, q_ref, kn_ref, vn_ref, ka_ref, kb_ref, va_ref, vb_ref, o_ref,
                      o_sc, m_sc, l_sc, g_sc, *, heads):
    n = pl.program_id(1)
    page = ka_ref.shape[0] // heads
    nq = q_ref.shape[0] // heads
    pad = 16
    nb = g_sc.shape[0]
    scale = HEAD_W ** -0.5

    def head_rows(ref, h):
        return ref[pl.ds(h, page, stride=heads), :]

    def q_head(h):
        return _pad_rows(q_ref[h * nq:(h + 1) * nq, :], pad)

    gates, scores = [], []
    for h in range(heads):
        qf = q_head(h)
        k = jnp.concatenate([head_rows(ka_ref, h), head_rows(kb_ref, h)], axis=0)
        gates.append(jnp.sum(qf * jnp.mean(k, axis=0, keepdims=True), axis=-1, keepdims=True))
        scores.append(_dot_nt(qf.astype(BF16), k.astype(BF16)) * scale)
    s = jnp.concatenate(scores, axis=0)
    m = jnp.max(s, axis=-1, keepdims=True)
    pr = jnp.exp(s - m)
    g_sc[n] = jnp.concatenate(gates, axis=0)
    m_sc[n] = m
    l_sc[n] = jnp.sum(pr, axis=-1, keepdims=True)
    prb = pr.astype(BF16)
    for h in range(heads):
        v = jnp.concatenate([head_rows(va_ref, h), head_rows(vb_ref, h)], axis=0).astype(BF16)
        o_sc[n, h * pad:(h + 1) * pad, :] = _dot(prb[h * pad:(h + 1) * pad], v)

    @pl.when(n == nb - 1)
    def _():
        n_rows = heads * pad
        n_io = lax.broadcasted_iota(I32, (nb, n_rows, 1), 0)
        gm = g_sc[...]
        sel = jnp.zeros(gm.shape, F32)
        for _ in range(MOBA_TOPK):
            mx = jnp.max(gm, axis=0, keepdims=True)
            idx = jnp.min(jnp.where(gm == mx, n_io, nb), axis=0, keepdims=True)
            hit = n_io == idx
            sel = jnp.where(hit, jnp.where(mx > NEG_INF, 1.0, 0.0), sel)
            gm = jnp.where(hit, NEG_INF, gm)
        s_own = jnp.concatenate(
            [_dot_nt(q_head(h).astype(BF16), _pad_rows(kn_ref[h * nq:(h + 1) * nq, :], LANES).astype(BF16))
             for h in range(heads)], axis=0) * scale
        rr = lax.broadcasted_iota(I32, s_own.shape, 0) & (pad - 1)
        cc = lax.broadcasted_iota(I32, s_own.shape, 1)
        s_own = jnp.where(cc <= rr, s_own, NEG_INF)
        m_own = jnp.max(s_own, axis=-1, keepdims=True)
        pr_own = jnp.exp(s_own - m_own)
        l_own = jnp.sum(pr_own, axis=-1, keepdims=True)
        prb_own = pr_own.astype(BF16)
        o_own = jnp.concatenate(
            [_dot(prb_own[h * pad:(h + 1) * pad], _pad_rows(vn_ref[h * nq:(h + 1) * nq, :], LANES).astype(BF16))
             for h in range(heads)], axis=0)
        m_all = m_sc[...]
        m_tot = jnp.maximum(m_own, jnp.max(jnp.where(sel > 0.0, m_all, NEG_INF), axis=0))
        w = jnp.where(sel > 0.0, jnp.exp(m_all - m_tot), 0.0)
        w_own = jnp.exp(m_own - m_tot)
        l_tot = w_own * l_own + jnp.sum(w * l_sc[...], axis=0)
        o_tot = (w_own * o_own + jnp.sum(w * o_sc[...], axis=0)) / l_tot
        for h in range(heads):
            o_ref[h * nq:(h + 1) * nq, :] = o_tot[h * pad:h * pad + nq]


def _moba_sample(page_table, q2, k2, v2, cache_k, cache_v, nq):
    n_seq, n_pages = page_table.shape
    n_pool, page, heads, _ = cache_k.shape
    assert MOBA_BLOCK == 2 * page and n_pages % 2 == 0 and nq <= 16
    nb = n_pages // 2
    n_rows = heads * 16
    row = pl.BlockSpec((nq * heads, HEAD_W), lambda s, n, pt: (s, 0))
    pg_a = pl.BlockSpec((None, page * heads, HEAD_W), lambda s, n, pt: (pt[s, 2 * n], 0, 0))
    pg_b = pl.BlockSpec((None, page * heads, HEAD_W), lambda s, n, pt: (pt[s, 2 * n + 1], 0, 0))
    ck = cache_k.reshape(n_pool, page * heads, HEAD_W)
    cv = cache_v.reshape(n_pool, page * heads, HEAD_W)
    return pl.pallas_call(
        functools.partial(_moba_sample_body, heads=heads),
        grid_spec=pltpu.PrefetchScalarGridSpec(
            num_scalar_prefetch=1,
            grid=(n_seq, nb),
            in_specs=[row, row, row, pg_a, pg_b, pg_a, pg_b],
            out_specs=row,
            scratch_shapes=[pltpu.VMEM((nb, n_rows, HEAD_W), F32),
                            pltpu.VMEM((nb, n_rows, 1), F32),
                            pltpu.VMEM((nb, n_rows, 1), F32),
                            pltpu.VMEM((nb, n_rows, 1), F32)]),
        out_shape=jax.ShapeDtypeStruct(q2.shape, F32),
        compiler_params=_params(("parallel", "arbitrary")),
        name="moba_sample",
    )(page_table, q2, k2, v2, ck, ck, cv, cv)


def _topk_rows(s, k_top):
    n = s.shape[0]
    row = lax.broadcasted_iota(I32, s.shape, 0)
    rank = jnp.full(s.shape, k_top, I32)
    vals, idxs = [], []
    for k in range(k_top):
        mx = jnp.max(s, axis=0, keepdims=True)
        idx = jnp.min(jnp.where(s == mx, row, n), axis=0, keepdims=True)
        hit = row == idx
        rank = jnp.where(hit, k, rank)
        s = jnp.where(hit, NEG_INF, s)
        vals.append(mx)
        idxs.append(idx)
    return jnp.concatenate(vals, axis=0), jnp.concatenate(idxs, axis=0), rank


def _peer_route_body(x_ref, wq_ref, sk_ref, e0_ref, thr_ref, e1_ref, r1_ref):
    kt = PEER_TOPK
    assert kt == 16, "the candidate-cell list below is written out for a 16 x 16 grid"
    x = x_ref[...]
    k_io = lax.broadcasted_iota(I32, (kt, x.shape[0]), 0)
    for h in range(PEER_HEADS):
        scores, tops = [], []
        for half in range(2):
            col = (h * 2 + half) * PEER_NKEYS
            q = _dot(x, wq_ref[:, col:col + PEER_NKEYS])
            s = _dot_nt(sk_ref[half], q.astype(BF16))
            scores.append(s)
            tops.append(_topk_rows(s, kt))
        (sv0, _, rank0), (sv1, _, rank1) = tops
        cand = jnp.concatenate([sv0[0:1] + sv1] + [sv0[a:a + 1] + sv1[0:8] for a in range(1, 8)]
                               + [sv0[8:16] + sv1[0:1]], axis=0)
        _, ci, _ = _topk_rows(cand, kt)
        cell_a = jnp.where(ci < 16, 0, jnp.where(ci < 72, 1 + ((ci - 16) >> 3), ci - 64))
        count = jnp.zeros(k_io.shape, F32)
        for j in range(kt):
            count = count + jnp.where(k_io == cell_a[j:j + 1], 1.0, 0.0)
        w0 = jnp.exp(sv0 - sv0[0:1])
        w1 = jnp.exp(sv1 - sv1[0:1])
        z = jnp.zeros((1, x.shape[0]), F32)
        for a in range(kt):
            inner = jnp.sum(jnp.where(k_io.astype(F32) < count[a:a + 1], w1, 0.0), axis=0, keepdims=True)
            z = z + w0[a:a + 1] * inner
        thr = jnp.zeros(rank0.shape, F32)
        for a in range(kt):
            thr = jnp.where(rank0 == a, count[a:a + 1], thr)
        e0_ref[h] = jnp.where(rank0 < kt, jnp.exp(scores[0] - sv0[0:1]), 0.0) / z
        thr_ref[h] = thr
        e1_ref[h] = jnp.where(rank1 < kt, jnp.exp(scores[1] - sv1[0:1]), 0.0)
        r1_ref[h] = rank1.astype(F32)


def _peer_route(xb, wqb, skb):
    m, d = xb.shape
    tb = TOKEN_TILE
    spec = pl.BlockSpec((PEER_HEADS, PEER_NKEYS, tb), lambda i: (0, 0, i))
    shape = jax.ShapeDtypeStruct((PEER_HEADS, PEER_NKEYS, m), F32)
    return pl.pallas_call(
        _peer_route_body,
        grid=(m // tb,),
        in_specs=[pl.BlockSpec((tb, d), lambda i: (i, 0)),
                  pl.BlockSpec(wqb.shape, lambda i: (0, 0)),
                  pl.BlockSpec(skb.shape, lambda i: (0, 0, 0))],
        out_specs=[spec] * 4,
        out_shape=[shape] * 4,
        compiler_params=_params(("parallel",)),
        name="peer_route",
    )(xb, wqb, skb)


def _gelu(x):
    return 0.5 * x * (1.0 + lax.erf(x * (2.0 ** -0.5)))


def _peer_expert_body(x_ref, e0_ref, thr_ref, e1_ref, r1_ref, u_ref, vt_ref, r_ref, g_ref, b_ref,
                      of_ref, ob_ref, acc_ref):
    c = pl.program_id(1)
    rows_per_chunk = u_ref.shape[0] // PEER_NKEYS

    @pl.when(c == 0)
    def _():
        acc_ref[...] = jnp.zeros(acc_ref.shape, F32)

    x = x_ref[...]
    parts = []
    for rr in range(rows_per_chunk):
        i0 = c * rows_per_chunk + rr
        hid = _dot_nt(u_ref[rr * PEER_NKEYS:(rr + 1) * PEER_NKEYS, :], x)
        thr_rows = [thr_ref[h, pl.ds(i0, 1), :] for h in range(PEER_HEADS)]
        e0_rows = [e0_ref[h, pl.ds(i0, 1), :] for h in range(PEER_HEADS)]
        halves = []
        for lt in range(x.shape[0] // LANES):
            cols = slice(lt * LANES, (lt + 1) * LANES)
            w = jnp.zeros((PEER_NKEYS, LANES), F32)
            for h in range(PEER_HEADS):
                keep = r1_ref[h, :, cols] < thr_rows[h][:, cols]
                w = w + jnp.where(keep, e1_ref[h, :, cols], 0.0) * e0_rows[h][:, cols]
            halves.append((_gelu(hid[:, cols]) * w).astype(BF16))
        parts.append(jnp.concatenate(halves, axis=1))
    aw = jnp.concatenate(parts, axis=0)
    acc_ref[...] += _dot(vt_ref[...], aw)

    @pl.when(c == pl.num_programs(1) - 1)
    def _():
        y = DN_ALPHA * r_ref[...] + acc_ref[...].T
        o = _layernorm_rows(y, g_ref[...], b_ref[...])
        of_ref[...] = o
        ob_ref[...] = o.astype(BF16)


def _peer_expert(xb, routes, ub, vtb, resid, g, b):
    m, d = xb.shape
    n_exp = ub.shape[0]
    tb = TOKEN_TILE
    ec = PEER_CHUNK
    rspec = pl.BlockSpec((PEER_HEADS, PEER_NKEYS, tb), lambda i, c: (0, 0, i))
    return pl.pallas_call(
        _peer_expert_body,
        grid=(m // tb, n_exp // ec),
        in_specs=[pl.BlockSpec((tb, d), lambda i, c: (i, 0)),
                  rspec, rspec, rspec, rspec,
                  pl.BlockSpec((ec, d), lambda i, c: (c, 0)),
                  pl.BlockSpec((d, ec), lambda i, c: (0, c)),
                  pl.BlockSpec((tb, d), lambda i, c: (i, 0)),
                  pl.BlockSpec((1, d), lambda i, c: (0, 0)),
                  pl.BlockSpec((1, d), lambda i, c: (0, 0))],
        out_specs=[pl.BlockSpec((tb, d), lambda i, c: (i, 0)), pl.BlockSpec((tb, d), lambda i, c: (i, 0))],
        out_shape=[jax.ShapeDtypeStruct((m, d), F32), jax.ShapeDtypeStruct((m, d), BF16)],
        scratch_shapes=[pltpu.VMEM((d, tb), F32)],
        compiler_params=_params(("parallel", "arbitrary")),
        name="peer_expert",
    )(xb, *routes, ub, vtb, resid, g.reshape(1, d), b.reshape(1, d))


def _peer_ln(h, hb, wqb, skb, ub, vtb, g, b):
    routes = _peer_route(hb, wqb, skb)
    return _peer_expert(hb, routes, ub, vtb, h, g, b)


def _token_head_rows(x):
    return x.reshape(-1, HEAD_W)


def _head_token_rows(x, n_seq):
    tokens = x.shape[0] // n_seq
    heads = x.shape[1] // HEAD_W
    return x.reshape(n_seq, tokens, heads, HEAD_W).transpose(0, 2, 1, 3).reshape(-1, HEAD_W)


def _rope_tables(pos, width):
    half = width // 2
    inv = ROPE_THETA ** (-jnp.arange(half, dtype=F32) / half)
    ang = pos.astype(F32)[:, None] * inv[None, :]
    cos = jnp.tile(jnp.cos(ang), (1, LANES // half))
    sin = jnp.sin(ang)
    sin = jnp.tile(jnp.concatenate([-sin, sin], axis=1), (1, LANES // width))
    return cos, sin


def kernel(x_prompt, x_sample, cache_diff_k, cache_diff_v, cache_sb_k, cache_sb_v, cache_moba_k,
           cache_moba_v, page_table, even_w_in, even_w_out, diff_lambda_q1, diff_lambda_k1,
           diff_lambda_q2, diff_lambda_k2, diff_subln_g, odd_w_in, odd_w_out, ln_mix_g, ln_mix_b,
           ln_ffn_g, ln_ffn_b, peer_w_q, peer_subkeys, peer_u, peer_v):
    batch, seq, d = x_prompt.shape
    n_seq, nq, _ = x_sample.shape
    page = cache_diff_k.shape[2]
    past_len = page_table.shape[1] * page
    depth = ln_mix_g.shape[0]
    assert depth == DEPTH and seq % ATTN_TILE == 0 and past_len % MOBA_BLOCK == 0
    assert (batch * seq) % TOKEN_TILE == 0 and (n_seq * nq) % TOKEN_TILE == 0 and nq == 8

    pos_p = jnp.tile(jnp.arange(seq, dtype=I32), batch)
    pos_s = jnp.tile(past_len + jnp.arange(nq, dtype=I32), n_seq)
    groups = [
        dict(h=x_prompt.reshape(batch * seq, d), pos=pos_p),
        dict(h=x_sample.reshape(n_seq * nq, d), pos=pos_s),
    ]
    for gr in groups:
        gr["hb"] = gr["h"].astype(BF16)
        gr["rope64"] = _rope_tables(gr["pos"], DIFF_DK)
        gr["rope128"] = _rope_tables(gr["pos"], HEAD_W)
    new_rows = [dict(), dict()]

    for layer in range(depth):
        li = layer // 2
        mix = []
        if layer % 2 == 0:
            lam_init = 0.8 - 0.6 * math.exp(-0.3 * layer)
            lam4 = jnp.stack([diff_lambda_q1[li], diff_lambda_k1[li], diff_lambda_q2[li], diff_lambda_k2[li]])
            sub_g = diff_subln_g[li].reshape(1, HEAD_W)
            w_in = even_w_in[li].astype(BF16)
            hw = DIFF_HEADS * HEAD_W
            w_cols = [w_in[:, j * hw:(j + 1) * hw] for j in range(6)]
            w_out = even_w_out[li].astype(BF16)
            ck = cache_diff_k[li].reshape(-1, page, DIFF_HEADS, HEAD_W)
            cv, sk, sv = cache_diff_v[li], cache_sb_k[li], cache_sb_v[li]
            for gi, gr in enumerate(groups):
                cos, sin = gr["rope64"]
                qd_f, qd_b = _proj(gr["hb"], w_cols[0], cos, sin, rope=DIFF_DK)
                kd_f, kd_b = _proj(gr["hb"], w_cols[1], cos, sin, rope=DIFF_DK)
                vd_f, vd_b = _proj(gr["hb"], w_cols[2], cos, sin)
                qs_f, qs_b = _proj(gr["hb"], w_cols[3], cos, sin)
                ks_f, ks_b = _proj(gr["hb"], w_cols[4], cos, sin)
                vs_f, vs_b = _proj(gr["hb"], w_cols[5], cos, sin)
                new_rows[gi].update(dk=kd_f, dv=vd_f, sk=ks_f, sv=vs_f)
                if gi == 0:
                    od = _diff_prompt(qd_b, kd_b, vd_b, lam4, sub_g, batch, seq, lam_init)
                    osb = _sb_prompt(qs_b, ks_b, vs_b, batch, seq)
                else:
                    od = _diff_sample(page_table, lam4, sub_g, _token_head_rows(qd_f), _token_head_rows(kd_f),
                                      _token_head_rows(vd_f), ck, cv, nq, lam_init).reshape(-1, hw)
                    osb = _sb_sample(page_table, _head_token_rows(qs_f, n_seq), _head_token_rows(ks_f, n_seq),
                                     _head_token_rows(vs_f, n_seq), sk, sv, nq)
                    osb = osb.reshape(n_seq, SB_HEADS, nq, HEAD_W).transpose(0, 2, 1, 3).reshape(-1, hw)
                mix.append([od, osb])
        else:
            w_in = odd_w_in[li].astype(BF16)
            hw = MOBA_HEADS * HEAD_W
            w_cols = [w_in[:, j * hw:(j + 1) * hw] for j in range(3)]
            w_out = odd_w_out[li].astype(BF16)
            mk, mv = cache_moba_k[li], cache_moba_v[li]
            for gi, gr in enumerate(groups):
                cos, sin = gr["rope128"]
                q_f, q_b = _proj(gr["hb"], w_cols[0], cos, sin, rope=HEAD_W)
                if gi == 0:
                    k_f, k_b, k_mean = _proj(gr["hb"], w_cols[1], cos, sin, rope=HEAD_W, want_mean=True)
                else:
                    k_f, k_b = _proj(gr["hb"], w_cols[1], cos, sin, rope=HEAD_W)
                v_f, v_b = _proj(gr["hb"], w_cols[2], cos, sin)
                new_rows[gi].update(mk=k_f, mv=v_f)
                if gi == 0:
                    k_mean = k_mean.reshape(batch, seq // MOBA_BLOCK, hw)
                    o = _moba_prompt(q_f, q_b, k_mean, k_b, v_b, batch, seq)
                else:
                    o = _moba_sample(page_table, _head_token_rows(q_f, n_seq), _head_token_rows(k_f, n_seq),
                                     _head_token_rows(v_f, n_seq), mk, mv, nq)
                    o = o.reshape(n_seq, MOBA_HEADS, nq, HEAD_W).transpose(0, 2, 1, 3).reshape(-1, hw)
                mix.append([o])
        wqb = peer_w_q[layer].astype(BF16)
        skb = peer_subkeys[layer].astype(BF16)
        ub = peer_u[layer].astype(BF16)
        vtb = peer_v[layer].astype(BF16).T
        for gi, gr in enumerate(groups):
            h, hb = _outproj_ln(mix[gi], w_out, gr["h"], ln_mix_g[layer], ln_mix_b[layer])
            gr["h"], gr["hb"] = _peer_ln(h, hb, wqb, skb, ub, vtb, ln_ffn_g[layer], ln_ffn_b[layer])

    def rows(gi, key, lead, tail):
        return new_rows[gi][key].reshape((1,) + lead + tail)

    lp, ls = (batch, seq), (n_seq, nq)
    t_dk = (DIFF_HEADS, 2, DIFF_DK)
    t_h8 = (DIFF_HEADS, HEAD_W)
    t_h16 = (MOBA_HEADS, HEAD_W)
    return (groups[0]["h"].reshape(batch, seq, d), groups[1]["h"].reshape(n_seq, nq, d),
            rows(0, "dk", lp, t_dk), rows(0, "dv", lp, t_h8), rows(0, "sk", lp, t_h8), rows(0, "sv", lp, t_h8),
            rows(0, "mk", lp, t_h16), rows(0, "mv", lp, t_h16),
            rows(1, "dk", ls, t_dk), rows(1, "dv", ls, t_h8), rows(1, "sk", ls, t_h8), rows(1, "sv", ls, t_h8),
            rows(1, "mk", ls, t_h16), rows(1, "mv", ls, t_h16))
```
